```python
import jax, jax.numpy as jnp
from jax import lax
import numpy as np

D_MODEL = 2048
BATCH = 1
SEQ = 8192
DEPTH = 1

CHUNK = 64
MEM_LEN = 256
MIX_WIDTH = D_MODEL
POOL_WIDTH = MIX_WIDTH // 2
CONV_WIDTH = MIX_WIDTH - POOL_WIDTH
POOL_WINDOWS = (2, 4, 8, 16)
N_POOL_GROUPS = len(POOL_WINDOWS)
POOL_GROUP_DIM = POOL_WIDTH // N_POOL_GROUPS
CONV_KERNEL = 31
XATTN_HEADS = 4
XATTN_HEAD_DIM = D_MODEL // XATTN_HEADS
N_GROUPS = 4
EXPERTS_PER_GROUP = 8
N_EXPERTS = N_GROUPS * EXPERTS_PER_GROUP
TOP_K = 2
D_EXPERT = D_MODEL // 4
MOE_BLOCK = 128
EPS = 1e-6

kernel_name = "hymba_pool_conformer_xmem_hmoe"


def rmsnorm(x, g):
    xf = x.astype(jnp.float32)
    y = xf * lax.rsqrt(jnp.mean(xf * xf, axis=-1, keepdims=True) + EPS)
    return (y * g.astype(jnp.float32)).astype(x.dtype)


def layernorm(x, g, b):
    xf = x.astype(jnp.float32)
    mu = jnp.mean(xf, axis=-1, keepdims=True)
    var = jnp.mean(jnp.square(xf - mu), axis=-1, keepdims=True)
    y = (xf - mu) * lax.rsqrt(var + EPS)
    return (y * g.astype(jnp.float32) + b.astype(jnp.float32)).astype(x.dtype)


def pool_mixer(u, w_pool, pool_scale):
    B, S, _ = u.shape
    ug = u.reshape(B, S, N_POOL_GROUPS, POOL_GROUP_DIM)
    cs = jnp.cumsum(ug.astype(jnp.float32), axis=1)
    cs0 = jnp.concatenate([jnp.zeros_like(cs[:, :1]), cs], axis=1)
    pos = jnp.arange(S)
    outs = []
    for g, w in enumerate(POOL_WINDOWS):
        c = cs0[:, :, g]
        lo = jnp.pad(c, ((0, 0), (w - 1, 0), (0, 0)))[:, :S]
        wsum = c[:, 1:] - lo
        cnt = jnp.minimum(pos + 1, w).astype(jnp.float32)[None, :, None]
        outs.append(wsum / cnt - ug[:, :, g].astype(jnp.float32))
    d = jnp.stack(outs, axis=2).astype(u.dtype)
    z = jnp.einsum('bsgc,gcd->bsgd', d, w_pool)
    return z.reshape(B, S, POOL_WIDTH) * pool_scale


def conv_module(v, w_dw, b_dw, ln_g, ln_b, w_pw):
    a, gate = jnp.split(v, 2, axis=-1)
    y = a * jax.nn.sigmoid(gate)
    y = lax.conv_general_dilated(y, w_dw, window_strides=(1,), padding=[(CONV_KERNEL - 1, 0)],
                                 dimension_numbers=('NWC', 'WIO', 'NWC'),
                                 feature_group_count=CONV_WIDTH)
    y = layernorm(y + b_dw, ln_g, ln_b)
    y = jax.nn.silu(y)
    return y @ w_pw


def cross_attention(h, m, w_q, w_kv, w_o):
    B, S, D = h.shape
    M = m.shape[1]
    q = (h @ w_q).reshape(B, S, XATTN_HEADS, XATTN_HEAD_DIM)
    k, v = jnp.split(m @ w_kv, 2, axis=-1)
    k = k.reshape(B, M, XATTN_HEADS, XATTN_HEAD_DIM)
    v = v.reshape(B, M, XATTN_HEADS, XATTN_HEAD_DIM)
    s = jnp.einsum('bshd,bmhd->bhsm', q, k).astype(jnp.float32) * (XATTN_HEAD_DIM ** -0.5)
    p = jax.nn.softmax(s, axis=-1).astype(v.dtype)
    o = jnp.einsum('bhsm,bmhd->bshd', p, v).reshape(B, S, D)
    return o @ w_o


def hierarchical_moe(h, w_rg, b_rg, w_re, b_re, w_gate, w_up, w_down):
    B, S, D = h.shape
    N = B * S
    hf = h.reshape(N, D)
    lg = (hf @ w_rg).astype(jnp.float32) + b_rg.astype(jnp.float32)
    pg = jax.nn.softmax(lg, axis=-1)
    pg_top, g_idx = lax.top_k(pg, 1)
    le = ((hf @ w_re).astype(jnp.float32) + b_re.astype(jnp.float32)).reshape(N, N_GROUPS, EXPERTS_PER_GROUP)
    le_sel = jnp.take_along_axis(le, g_idx[:, :, None], axis=1)[:, 0]
    pe = jax.nn.softmax(le_sel, axis=-1)
    pe_top, e_loc = lax.top_k(pe, TOP_K)
    w_tok = pg_top * pe_top / jnp.sum(pe_top, axis=-1, keepdims=True)
    e_glob = g_idx * EXPERTS_PER_GROUP + e_loc
    A = N * TOP_K
    flat_e = e_glob.reshape(A)
    flat_tok = jnp.repeat(jnp.arange(N, dtype=jnp.int32), TOP_K)
    flat_w = w_tok.reshape(A)
    order = jnp.argsort(flat_e)
    se, st, sw = flat_e[order], flat_tok[order], flat_w[order]
    counts = jnp.bincount(flat_e, length=N_EXPERTS)
    starts = jnp.cumsum(counts) - counts
    padded = (counts + MOE_BLOCK - 1) // MOE_BLOCK * MOE_BLOCK
    pends = jnp.cumsum(padded)
    pstarts = pends - padded
    dest = pstarts[se] + (jnp.arange(A) - starts[se])
    nb = (A + MOE_BLOCK - 1) // MOE_BLOCK + N_EXPERTS
    P = nb * MOE_BLOCK
    slot_tok = jnp.zeros((P,), jnp.int32).at[dest].set(st)
    slot_w = jnp.zeros((P,), jnp.float32).at[dest].set(sw)
    block_e = jnp.clip(jnp.searchsorted(pends, jnp.arange(nb) * MOE_BLOCK, side='right'), 0, N_EXPERTS - 1)
    xs = hf[slot_tok].reshape(nb, MOE_BLOCK, D)

    def expert_block(args):
        xb, e = args
        hid = jax.nn.silu(xb @ w_gate[e]) * (xb @ w_up[e])
        return hid @ w_down[e]

    ys = lax.map(expert_block, (xs, block_e)).reshape(P, D)
    ys = ys * slot_w[:, None].astype(ys.dtype)
    out = jnp.zeros((N, D), h.dtype).at[slot_tok].add(ys)
    return out.reshape(B, S, D)


def setup_inputs(seed: int = 0) -> dict:
    key = jax.random.key(seed)
    ks = jax.random.split(key, 32)
    f32 = jnp.float32
    L = DEPTH

    def w(k, shape, fan_in):
        return jax.random.normal(k, shape, f32) * (fan_in ** -0.5)

    def gain(k, shape):
        return 1.0 + 0.02 * jax.random.normal(k, shape, f32)

    return {
        "x": jax.random.normal(ks[0], (BATCH, SEQ, D_MODEL), f32),
        "mem": jax.random.normal(ks[1], (BATCH, MEM_LEN, D_MODEL), f32),
        "norm_mix_g": gain(ks[2], (L, D_MODEL)),
        "w_in": w(ks[3], (L, D_MODEL, POOL_WIDTH + 2 * CONV_WIDTH), D_MODEL),
        "w_pool": w(ks[4], (L, N_POOL_GROUPS, POOL_GROUP_DIM, POOL_GROUP_DIM), POOL_GROUP_DIM),
        "pool_scale": gain(ks[5], (L, POOL_WIDTH)),
        "w_dw": w(ks[6], (L, CONV_KERNEL, 1, CONV_WIDTH), CONV_KERNEL),
        "b_dw": 0.02 * jax.random.normal(ks[7], (L, CONV_WIDTH), f32),
        "conv_ln_g": gain(ks[8], (L, CONV_WIDTH)),
        "conv_ln_b": 0.02 * jax.random.normal(ks[9], (L, CONV_WIDTH), f32),
        "w_conv_pw": w(ks[10], (L, CONV_WIDTH, CONV_WIDTH), CONV_WIDTH),
        "out_norm_a_g": gain(ks[11], (L, POOL_WIDTH)),
        "out_norm_b_g": gain(ks[12], (L, CONV_WIDTH)),
        "w_out": w(ks[13], (L, MIX_WIDTH, D_MODEL), MIX_WIDTH),
        "norm_xattn_g": gain(ks[14], (L, D_MODEL)),
        "norm_mem_g": gain(ks[15], (L, D_MODEL)),
        "w_q_mem": w(ks[16], (L, D_MODEL, D_MODEL), D_MODEL),
        "w_kv_mem": w(ks[17], (L, D_MODEL, 2 * D_MODEL), D_MODEL),
        "w_o_mem": w(ks[18], (L, D_MODEL, D_MODEL), D_MODEL),
        "norm_ffn_g": gain(ks[19], (L, D_MODEL)),
        "w_router_group": w(ks[20], (L, D_MODEL, N_GROUPS), D_MODEL),
        "b_router_group": 0.01 * jax.random.normal(ks[21], (L, N_GROUPS), f32),
        "w_router_expert": w(ks[22], (L, D_MODEL, N_EXPERTS), D_MODEL),
        "b_router_expert": 0.01 * jax.random.normal(ks[23], (L, N_EXPERTS), f32),
        "w_exp_gate": w(ks[24], (L, N_EXPERTS, D_MODEL, D_EXPERT), D_MODEL),
        "w_exp_up": w(ks[25], (L, N_EXPERTS, D_MODEL, D_EXPERT), D_MODEL),
        "w_exp_down": w(ks[26], (L, N_EXPERTS, D_EXPERT, D_MODEL), D_EXPERT),
        "final_norm_g": gain(ks[27], (D_MODEL,)),
    }


def reference(x, mem, norm_mix_g, w_in, w_pool, pool_scale, w_dw, b_dw, conv_ln_g, conv_ln_b,
              w_conv_pw, out_norm_a_g, out_norm_b_g, w_out, norm_xattn_g, norm_mem_g, w_q_mem,
              w_kv_mem, w_o_mem, norm_ffn_g, w_router_group, b_router_group, w_router_expert,
              b_router_expert, w_exp_gate, w_exp_up, w_exp_down, final_norm_g):
    for l in range(DEPTH):
        h = rmsnorm(x, norm_mix_g[l])
        proj = h @ w_in[l]
        ya = pool_mixer(proj[..., :POOL_WIDTH], w_pool[l], pool_scale[l])
        yb = conv_module(proj[..., POOL_WIDTH:], w_dw[l], b_dw[l], conv_ln_g[l], conv_ln_b[l], w_conv_pw[l])
        y = jnp.concatenate([rmsnorm(ya, out_norm_a_g[l]), rmsnorm(yb, out_norm_b_g[l])], axis=-1)
        x = x + y @ w_out[l]
        h = rmsnorm(x, norm_xattn_g[l])
        m = rmsnorm(mem, norm_mem_g[l])
        x = x + cross_attention(h, m, w_q_mem[l], w_kv_mem[l], w_o_mem[l])
        h = rmsnorm(x, norm_ffn_g[l])
        x = x + hierarchical_moe(h, w_router_group[l], b_router_group[l], w_router_expert[l],
                                 b_router_expert[l], w_exp_gate[l], w_exp_up[l], w_exp_down[l])
    return rmsnorm(x, final_norm_g)
```

```python
import functools

import jax
import jax.numpy as jnp
from jax import lax
from jax.experimental import pallas as pl
from jax.experimental.pallas import tpu as pltpu

F32 = jnp.float32
BF16 = jnp.bfloat16
I32 = jnp.int32

EPS = 1e-6
POOL_WINDOWS = (2, 4, 8, 16)
CONV_KERNEL = 31
XATTN_HEADS = 4
N_GROUPS = 4
EXPERTS_PER_GROUP = 8
N_EXPERTS = N_GROUPS * EXPERTS_PER_GROUP
TOP_K = 2

SUBLANES = 8
LANES = 128
VMEM_LIMIT_BYTES = 56 * 1024 * 1024

POOL_HALO = 16
CONV_HALO = 32
MIXER_TILE = 256
XATTN_TILE = 256
ROUTE_TILE = 512
MOE_BLOCK = 128
COMBINE_TILE = 256
LOGIT_ROWS = 128
ROUTE_ROWS = 40


def _rms(x, g):
    return x * lax.rsqrt(jnp.mean(x * x, axis=-1, keepdims=True) + EPS) * g


def _resident(shape):
    nd = len(shape)
    return pl.BlockSpec(shape, lambda *_: (0,) * nd, pipeline_mode=pl.Buffered(1))


def _mixer_kernel(x_ref, gmix_ref, win_ref, wpool_ref, pscale_ref, wdw_ref, bdw_ref, lng_ref, lnb_ref,
                  wpw_ref, ga_ref, gb_ref, wout_ref, o_ref, ubuf, ybuf, cbuf):
    i = pl.program_id(0)
    tm = x_ref.shape[0]
    pw = ubuf.shape[1]
    cw = ybuf.shape[1]
    gdim = pw // len(POOL_WINDOWS)

    @pl.when(i == 0)
    def _():
        ubuf[0:POOL_HALO, :] = jnp.zeros((POOL_HALO, pw), F32)
        ybuf[0:CONV_HALO, :] = jnp.zeros((CONV_HALO, cw), F32)

    x = x_ref[...]
    h = _rms(x, gmix_ref[...])
    proj = jnp.dot(h.astype(BF16), win_ref[...], preferred_element_type=F32)

    u = proj[:, :pw]
    ubuf[POOL_HALO:POOL_HALO + tm, :] = u
    pos = i * tm + lax.broadcasted_iota(I32, (tm, 1), 0)
    zs = []
    for g, w in enumerate(POOL_WINDOWS):
        c0 = g * gdim
        wsum = ubuf[POOL_HALO:POOL_HALO + tm, c0:c0 + gdim]
        for j in range(1, w):
            wsum = wsum + ubuf[POOL_HALO - j:POOL_HALO - j + tm, c0:c0 + gdim]
        cnt = jnp.minimum(pos + 1, w).astype(F32)
        d = wsum / cnt - u[:, c0:c0 + gdim]
        zs.append(jnp.dot(d.astype(BF16), wpool_ref[g], preferred_element_type=F32))
    ya = jnp.concatenate(zs, axis=-1) * pscale_ref[...]
    ubuf[0:POOL_HALO, :] = ubuf[tm:tm + POOL_HALO, :]

    a = proj[:, pw:pw + cw]
    gate = proj[:, pw + cw:pw + 2 * cw]
    ybuf[CONV_HALO:CONV_HALO + tm, :] = a * (1.0 / (1.0 + jnp.exp(-gate)))
    row_chunk = 64
    col_chunk = 256
    first = CONV_HALO - (CONV_KERNEL - 1)
    for r0 in range(0, tm, row_chunk):
        for c0 in range(0, cw, col_chunk):
            acc = None
            for k in range(CONV_KERNEL):
                seg = ybuf[first + r0 + k:first + r0 + k + row_chunk, c0:c0 + col_chunk]
                term = seg * wdw_ref[k:k + 1, c0:c0 + col_chunk]
                acc = term if acc is None else acc + term
            cbuf[r0:r0 + row_chunk, c0:c0 + col_chunk] = acc
    ybuf[0:CONV_HALO, :] = ybuf[tm:tm + CONV_HALO, :]
    c = cbuf[...] + bdw_ref[...]
    mu = jnp.mean(c, axis=-1, keepdims=True)
    var = jnp.mean(jnp.square(c - mu), axis=-1, keepdims=True)
    c = (c - mu) * lax.rsqrt(var + EPS) * lng_ref[...] + lnb_ref[...]
    c = c * (1.0 / (1.0 + jnp.exp(-c)))
    yb = jnp.dot(c.astype(BF16), wpw_ref[...], preferred_element_type=F32)

    y = jnp.concatenate([_rms(ya, ga_ref[...]), _rms(yb, gb_ref[...])], axis=-1)
    o_ref[...] = x + jnp.dot(y.astype(BF16), wout_ref[...], preferred_element_type=F32)


def _mixer(x, g_mix, w_in, w_pool, pool_scale, w_dw, b_dw, ln_g, ln_b, w_pw, g_a, g_b, w_out):
    n, d = x.shape
    pw = w_pool.shape[0] * w_pool.shape[1]
    cw = w_pw.shape[0]
    tm = min(MIXER_TILE, n)
    row = pl.BlockSpec((tm, d), lambda i: (i, 0))
    return pl.pallas_call(
        _mixer_kernel,
        grid=(n // tm,),
        in_specs=[row, _resident(g_mix.shape), _resident(w_in.shape), _resident(w_pool.shape),
                  _resident(pool_scale.shape), _resident(w_dw.shape), _resident(b_dw.shape),
                  _resident(ln_g.shape), _resident(ln_b.shape), _resident(w_pw.shape),
                  _resident(g_a.shape), _resident(g_b.shape), _resident(w_out.shape)],
        out_specs=row,
        out_shape=jax.ShapeDtypeStruct((n, d), F32),
        scratch_shapes=[pltpu.VMEM((tm + POOL_HALO, pw), F32),
                        pltpu.VMEM((tm + CONV_HALO, cw), F32),
                        pltpu.VMEM((tm, cw), F32)],
        compiler_params=pltpu.CompilerParams(dimension_semantics=("arbitrary",),
                                             vmem_limit_bytes=VMEM_LIMIT_BYTES),
        name="mixer",
    )(x, g_mix, w_in, w_pool, pool_scale, w_dw, b_dw, ln_g, ln_b, w_pw, g_a, g_b, w_out)


def _kv_kernel(mem_ref, g_ref, w_ref, o_ref):
    m = _rms(mem_ref[...], g_ref[...])
    o_ref[...] = jnp.dot(m.astype(BF16), w_ref[...].astype(BF16), preferred_element_type=F32).astype(BF16)


def _kv(mem, g_mem, w_kv):
    m, d = mem.shape
    n_out = w_kv.shape[1]
    bn = 512
    return pl.pallas_call(
        _kv_kernel,
        grid=(n_out // bn,),
        in_specs=[_resident((m, d)), _resident(g_mem.shape), pl.BlockSpec((d, bn), lambda j: (0, j))],
        out_specs=pl.BlockSpec((m, bn), lambda j: (0, j)),
        out_shape=jax.ShapeDtypeStruct((m, n_out), BF16),
        compiler_params=pltpu.CompilerParams(dimension_semantics=("arbitrary",),
                                             vmem_limit_bytes=VMEM_LIMIT_BYTES),
        name="kv",
    )(mem, g_mem, w_kv)


def _xattn_kernel(x_ref, gx_ref, wq_ref, kv_ref, wo_ref, gf_ref, wr_ref, o_ref, lg_ref):
    d = x_ref.shape[1]
    hd = d // XATTN_HEADS
    x = x_ref[...]
    h = _rms(x, gx_ref[...])
    q = jnp.dot(h.astype(BF16), wq_ref[...], preferred_element_type=F32)
    outs = []
    for hh in range(XATTN_HEADS):
        qh = q[:, hh * hd:(hh + 1) * hd].astype(BF16)
        kh = kv_ref[:, hh * hd:(hh + 1) * hd]
        vh = kv_ref[:, d + hh * hd:d + (hh + 1) * hd]
        s = lax.dot_general(qh, kh, (((1,), (1,)), ((), ())), preferred_element_type=F32) * (hd ** -0.5)
        e = jnp.exp(s - jnp.max(s, axis=-1, keepdims=True))
        p = e / jnp.sum(e, axis=-1, keepdims=True)
        outs.append(jnp.dot(p.astype(BF16), vh, preferred_element_type=F32))
    o = jnp.concatenate(outs, axis=-1)
    x2 = x + jnp.dot(o.astype(BF16), wo_ref[...], preferred_element_type=F32)
    o_ref[...] = x2
    h3 = _rms(x2, gf_ref[...])
    lg_ref[...] = lax.dot_general(wr_ref[...], h3.astype(BF16), (((1,), (1,)), ((), ())),
                                  preferred_element_type=F32)


def _xattn(x, g_x, w_q, kv, w_o, g_ffn, w_router_t):
    n, d = x.shape
    tm = min(XATTN_TILE, n)
    row = pl.BlockSpec((tm, d), lambda i: (i, 0))
    return pl.pallas_call(
        _xattn_kernel,
        grid=(n // tm,),
        in_specs=[row, _resident(g_x.shape), _resident(w_q.shape), _resident(kv.shape), _resident(w_o.shape),
                  _resident(g_ffn.shape), _resident(w_router_t.shape)],
        out_specs=[row, pl.BlockSpec((LOGIT_ROWS, tm), lambda i: (0, i))],
        out_shape=[jax.ShapeDtypeStruct((n, d), F32), jax.ShapeDtypeStruct((LOGIT_ROWS, n), F32)],
        compiler_params=pltpu.CompilerParams(dimension_semantics=("arbitrary",),
                                             vmem_limit_bytes=VMEM_LIMIT_BYTES),
        name="xattn",
    )(x, g_x, w_q, kv, w_o, g_ffn, w_router_t)


def _first_argmax(v, rows):
    m = jnp.max(v, axis=0, keepdims=True)
    idx = jnp.min(jnp.where(v == m, rows.astype(F32), float(v.shape[0])), axis=0, keepdims=True)
    return m, idx.astype(I32)


def _route_kernel(lg_ref, bias_ref, tri_ref, etri_ref, dest_ref, wtok_ref, blk_ref, eg_buf, rank_buf, *, block):
    n = lg_ref.shape[1]
    t = tri_ref.shape[0]
    n_tiles = n // t
    epg = EXPERTS_PER_GROUP
    rows_g = lax.broadcasted_iota(I32, (N_GROUPS, t), 0)
    rows_e = lax.broadcasted_iota(I32, (epg, t), 0)
    rows_all = lax.broadcasted_iota(I32, (N_EXPERTS, t), 0)

    def tile_body(j, carry):
        sl = pl.ds(pl.multiple_of(j * t, t), t)
        le = lg_ref[0:N_EXPERTS, sl] + bias_ref[0:N_EXPERTS, :]
        lgp = lg_ref[N_EXPERTS:N_EXPERTS + N_GROUPS, sl] + bias_ref[N_EXPERTS:N_EXPERTS + N_GROUPS, :]
        eg_ = jnp.exp(lgp - jnp.max(lgp, axis=0, keepdims=True))
        pg = eg_ / jnp.sum(eg_, axis=0, keepdims=True)
        pg_top, g_idx = _first_argmax(pg, rows_g)
        le_sel = jnp.zeros((epg, t), F32)
        for g in range(N_GROUPS):
            le_sel = jnp.where(g_idx == g, le[g * epg:(g + 1) * epg, :], le_sel)
        ee = jnp.exp(le_sel - jnp.max(le_sel, axis=0, keepdims=True))
        pe = ee / jnp.sum(ee, axis=0, keepdims=True)
        p1, i1 = _first_argmax(pe, rows_e)
        p2, i2 = _first_argmax(jnp.where(rows_e == i1, -1.0, pe), rows_e)
        psum = p1 + p2
        w1 = pg_top * p1 / psum
        w2 = pg_top * p2 / psum
        e1 = g_idx * epg + i1
        e2 = g_idx * epg + i2
        oh1 = rows_all == e1
        oh2 = rows_all == e2
        onehot = jnp.where(oh1 | oh2, 1.0, 0.0)
        prefix = carry + jnp.dot(onehot.astype(BF16), tri_ref[...], preferred_element_type=F32)
        r1 = jnp.sum(jnp.where(oh1, prefix, 0.0), axis=0, keepdims=True)
        r2 = jnp.sum(jnp.where(oh2, prefix, 0.0), axis=0, keepdims=True)
        eg_buf[0:1, sl] = e1
        eg_buf[1:2, sl] = e2
        rank_buf[0:1, sl] = r1
        rank_buf[1:2, sl] = r2
        wtok_ref[0:1, sl] = w1
        wtok_ref[1:2, sl] = w2
        return carry + jnp.sum(onehot, axis=1, keepdims=True)

    counts = lax.fori_loop(0, n_tiles, tile_body, jnp.zeros((N_EXPERTS, 1), F32))
    wtok_ref[2:SUBLANES, :] = jnp.zeros((SUBLANES - 2, n), F32)

    nblk = jnp.floor((counts + (block - 1)) * (1.0 / block))
    nblk_b = jnp.broadcast_to(nblk, (N_EXPERTS, LANES)).astype(BF16)
    bstart = jnp.dot(etri_ref[...], nblk_b, preferred_element_type=F32)[:, 0:1]
    bend = bstart + nblk
    pstart = bstart * block

    def dest_body(j, _):
        sl = pl.ds(pl.multiple_of(j * t, t), t)
        for k in range(TOP_K):
            oh = rows_all == eg_buf[k:k + 1, sl]
            base = jnp.sum(jnp.where(oh, pstart, 0.0), axis=0, keepdims=True)
            dest_ref[k:k + 1, sl] = (base + rank_buf[k:k + 1, sl]).astype(I32)
        return 0

    lax.fori_loop(0, n_tiles, dest_body, 0)
    dest_ref[2:SUBLANES, :] = jnp.zeros((SUBLANES - 2, n), I32)

    nbp = blk_ref.shape[1]
    bidx = lax.broadcasted_iota(I32, (N_EXPERTS, nbp), 1).astype(F32)
    be = jnp.sum(jnp.where(bend <= bidx, 1.0, 0.0), axis=0, keepdims=True)
    n_used = jnp.max(bend, axis=0, keepdims=True)
    blk_ref[0:1, :] = be.astype(I32)
    blk_ref[1:2, :] = jnp.broadcast_to(n_used, (1, nbp)).astype(I32)
    blk_ref[2:SUBLANES, :] = jnp.zeros((SUBLANES - 2, nbp), I32)


def _route(logits_t, bias, block, nb):
    n = logits_t.shape[1]
    t = min(ROUTE_TILE, n)
    nbp = pl.cdiv(nb, LANES) * LANES
    assert nbp <= 256, "block counts must stay exactly representable in bf16"
    tri = (lax.broadcasted_iota(I32, (t, t), 0) < lax.broadcasted_iota(I32, (t, t), 1)).astype(BF16)
    etri = (lax.broadcasted_iota(I32, (N_EXPERTS, N_EXPERTS), 1)
            < lax.broadcasted_iota(I32, (N_EXPERTS, N_EXPERTS), 0)).astype(BF16)
    return pl.pallas_call(
        functools.partial(_route_kernel, block=block),
        grid=(1,),
        in_specs=[pl.BlockSpec((ROUTE_ROWS, n), lambda i: (0, 0)), _resident(bias.shape), _resident(tri.shape),
                  _resident(etri.shape)],
        out_specs=[pl.BlockSpec((SUBLANES, n), lambda i: (0, 0)), pl.BlockSpec((SUBLANES, n), lambda i: (0, 0)),
                   pl.BlockSpec((SUBLANES, nbp), lambda i: (0, 0))],
        out_shape=[jax.ShapeDtypeStruct((SUBLANES, n), I32), jax.ShapeDtypeStruct((SUBLANES, n), F32),
                   jax.ShapeDtypeStruct((SUBLANES, nbp), I32)],
        scratch_shapes=[pltpu.VMEM((SUBLANES, n), I32), pltpu.VMEM((SUBLANES, n), F32)],
        compiler_params=pltpu.CompilerParams(dimension_semantics=("arbitrary",),
                                             vmem_limit_bytes=VMEM_LIMIT_BYTES),
        name="route",
    )(logits_t, bias, tri, etri)


def _gather_rows(idx_ref, src_hbm, dst, sem, rows):
    def body(r, _):
        pltpu.make_async_copy(src_hbm.at[pl.ds(idx_ref[0, 0, r], 1)], dst.at[pl.ds(r, 1)], sem).start()
        return 0
    lax.fori_loop(0, rows, body, 0)


def _wait_rows(src_hbm, dst, sem, rows):
    def body(r, _):
        pltpu.make_async_copy(src_hbm.at[pl.ds(0, 1)], dst.at[pl.ds(r, 1)], sem).wait()
        return 0
    lax.fori_loop(0, rows, body, 0)


def _experts_kernel(be_ref, nu_ref, tok_ref, tokn_ref, x_hbm, gf_ref, wg_ref, wu_ref, wd_ref, o_ref,
                    xbuf, sems, wg16, wu16, wd16):
    b = pl.program_id(0)
    nb = pl.num_programs(0)
    rows = xbuf.shape[1]
    slot = b % 2
    n_used = nu_ref[0]

    @pl.when(b == 0)
    def _():
        _gather_rows(tok_ref, x_hbm, xbuf.at[0], sems.at[0], rows)

    @pl.when(jnp.logical_and(b + 1 < n_used, b + 1 < nb))
    def _():
        _gather_rows(tokn_ref, x_hbm, xbuf.at[1 - slot], sems.at[1 - slot], rows)

    @pl.when(jnp.logical_or(b == 0, be_ref[b] != be_ref[jnp.maximum(b - 1, 0)]))
    def _():
        wg16[...] = wg_ref[...].astype(BF16)
        wu16[...] = wu_ref[...].astype(BF16)
        wd16[...] = wd_ref[...].astype(BF16)

    @pl.when(jnp.logical_or(b == 0, b < n_used))
    def _():
        _wait_rows(x_hbm, xbuf.at[slot], sems.at[slot], rows)

    @pl.when(b < n_used)
    def _():
        h = _rms(xbuf[slot], gf_ref[...]).astype(BF16)
        hg = jnp.dot(h, wg16[...], preferred_element_type=F32)
        hu = jnp.dot(h, wu16[...], preferred_element_type=F32)
        hid = hg * (1.0 / (1.0 + jnp.exp(-hg))) * hu
        o_ref[...] = jnp.dot(hid.astype(BF16), wd16[...], preferred_element_type=F32)

    @pl.when(b >= n_used)
    def _():
        o_ref[...] = jnp.zeros(o_ref.shape, F32)


def _experts(block_e, n_used, slot_tok, x2, g_ffn, w_gate, w_up, w_down, block):
    nb = slot_tok.shape[0]
    n, d = x2.shape
    de = w_gate.shape[2]
    smem_blk = lambda f: pl.BlockSpec((1, 1, block), f, memory_space=pltpu.SMEM)
    grid_spec = pltpu.PrefetchScalarGridSpec(
        num_scalar_prefetch=2,
        grid=(nb,),
        in_specs=[smem_blk(lambda b, be, nu: (b, 0, 0)),
                  smem_blk(lambda b, be, nu: (jnp.minimum(b + 1, nb - 1), 0, 0)),
                  pl.BlockSpec(memory_space=pl.ANY),
                  _resident(g_ffn.shape),
                  pl.BlockSpec((None, d, de), lambda b, be, nu: (be[b], 0, 0)),
                  pl.BlockSpec((None, d, de), lambda b, be, nu: (be[b], 0, 0)),
                  pl.BlockSpec((None, de, d), lambda b, be, nu: (be[b], 0, 0))],
        out_specs=pl.BlockSpec((block, d), lambda b, be, nu: (b, 0)),
        scratch_shapes=[pltpu.VMEM((2, block, d), F32), pltpu.SemaphoreType.DMA((2,)),
                        pltpu.VMEM((d, de), BF16), pltpu.VMEM((d, de), BF16), pltpu.VMEM((de, d), BF16)],
    )
    return pl.pallas_call(
        _experts_kernel,
        grid_spec=grid_spec,
        out_shape=jax.ShapeDtypeStruct((nb * block, d), F32),
        compiler_params=pltpu.CompilerParams(dimension_semantics=("arbitrary",),
                                             vmem_limit_bytes=VMEM_LIMIT_BYTES),
        name="experts",
    )(block_e, n_used, slot_tok, slot_tok, x2, g_ffn, w_gate, w_up, w_down)


def _combine_kernel(d_ref, dn_ref, x_ref, w_ref, y_hbm, g_ref, o_ref, ybuf, sems):
    i = pl.program_id(0)
    nt = pl.num_programs(0)
    tm = x_ref.shape[0]
    slot = i % 2

    def start(idx_ref, s):
        for k in range(TOP_K):
            def body(r, _, k=k):
                pltpu.make_async_copy(y_hbm.at[pl.ds(idx_ref[0, 0, k * tm + r], 1)],
                                      ybuf.at[s, k, pl.ds(r, 1)], sems.at[s]).start()
                return 0
            lax.fori_loop(0, tm, body, 0)

    @pl.when(i == 0)
    def _():
        start(d_ref, 0)

    @pl.when(i + 1 < nt)
    def _():
        start(dn_ref, 1 - slot)

    for k in range(TOP_K):
        _wait_rows(y_hbm, ybuf.at[slot, k], sems.at[slot], tm)
    w = w_ref[...]
    y = x_ref[...] + (ybuf[slot, 0] * w[:, 0:1] + ybuf[slot, 1] * w[:, 1:2])
    o_ref[...] = _rms(y, g_ref[...])


def _combine(dest_tiles, x2, w_tok, ys, g_final):
    n, d = x2.shape
    tm = dest_tiles.shape[2] // TOP_K
    nt = n // tm
    smem_blk = lambda f: pl.BlockSpec((1, 1, TOP_K * tm), f, memory_space=pltpu.SMEM)
    row = pl.BlockSpec((tm, d), lambda i: (i, 0))
    return pl.pallas_call(
        _combine_kernel,
        grid=(nt,),
        in_specs=[smem_blk(lambda i: (i, 0, 0)), smem_blk(lambda i: (jnp.minimum(i + 1, nt - 1), 0, 0)),
                  row, pl.BlockSpec((tm, SUBLANES), lambda i: (i, 0)), pl.BlockSpec(memory_space=pl.ANY),
                  _resident(g_final.shape)],
        out_specs=row,
        out_shape=jax.ShapeDtypeStruct((n, d), F32),
        scratch_shapes=[pltpu.VMEM((2, TOP_K, tm, d), F32), pltpu.SemaphoreType.DMA((2,))],
        compiler_params=pltpu.CompilerParams(dimension_semantics=("arbitrary",),
                                             vmem_limit_bytes=VMEM_LIMIT_BYTES),
        name="combine",
    )(dest_tiles, dest_tiles, x2, w_tok, ys, g_final)


def _layer(x, mem, norm_mix_g, w_in, w_pool, pool_scale, w_dw, b_dw, conv_ln_g, conv_ln_b, w_conv_pw,
           out_norm_a_g, out_norm_b_g, w_out, norm_xattn_g, norm_mem_g, w_q_mem, w_kv_mem, w_o_mem, norm_ffn_g,
           w_router_group, b_router_group, w_router_expert, b_router_expert, w_exp_gate, w_exp_up, w_exp_down,
           out_gain):
    n, d = x.shape
    r2 = lambda v: v.reshape(1, -1)
    cw = w_conv_pw.shape[0]
    x1 = _mixer(x, r2(norm_mix_g), w_in.astype(BF16), w_pool.astype(BF16), r2(pool_scale),
                w_dw.reshape(CONV_KERNEL, cw), r2(b_dw),
                r2(conv_ln_g), r2(conv_ln_b), w_conv_pw.astype(BF16), r2(out_norm_a_g), r2(out_norm_b_g),
                w_out.astype(BF16))

    kv = _kv(mem, r2(norm_mem_g), w_kv_mem)
    w_router_t = jnp.zeros((LOGIT_ROWS, d), F32)
    w_router_t = w_router_t.at[:N_EXPERTS].set(w_router_expert.T).at[N_EXPERTS:N_EXPERTS + N_GROUPS].set(
        w_router_group.T).astype(BF16)
    x2, logits_t = _xattn(x1, r2(norm_xattn_g), w_q_mem.astype(BF16), kv, w_o_mem.astype(BF16), r2(norm_ffn_g),
                          w_router_t)

    block = MOE_BLOCK
    nb = pl.cdiv(n * TOP_K, block) + N_EXPERTS
    bias = jnp.zeros((ROUTE_ROWS, 1), F32)
    bias = bias.at[:N_EXPERTS, 0].set(b_router_expert).at[N_EXPERTS:N_EXPERTS + N_GROUPS, 0].set(b_router_group)
    dest, w_tok, blk = _route(logits_t, bias, block, nb)
    dest = dest[:TOP_K]
    n_used = blk[1, :1]
    bidx = jnp.arange(nb, dtype=I32)
    block_e = jnp.minimum(blk[0, :nb], N_EXPERTS - 1)
    block_e = jnp.where(bidx < n_used[0], block_e, block_e[jnp.maximum(n_used[0] - 1, 0)])
    tok = jnp.broadcast_to(jnp.arange(n, dtype=I32), (TOP_K, n))
    slot_tok = jnp.zeros((nb * block,), I32).at[dest.reshape(-1)].set(tok.reshape(-1)).reshape(nb, 1, block)

    ys = _experts(block_e, n_used, slot_tok, x2, r2(norm_ffn_g), w_exp_gate, w_exp_up, w_exp_down, block)

    tm = min(COMBINE_TILE, n)
    dest_tiles = dest.reshape(TOP_K, n // tm, tm).transpose(1, 0, 2).reshape(n // tm, 1, TOP_K * tm)
    return _combine(dest_tiles, x2, w_tok.T, ys, r2(out_gain))


def kernel(x, mem, norm_mix_g, w_in, w_pool, pool_scale, w_dw, b_dw, conv_ln_g, conv_ln_b, w_conv_pw, out_norm_a_g,
           out_norm_b_g, w_out, norm_xattn_g, norm_mem_g, w_q_mem, w_kv_mem, w_o_mem, norm_ffn_g, w_router_group,
           b_router_group, w_router_expert, b_router_expert, w_exp_gate, w_exp_up, w_exp_down, final_norm_g):
    assert x.shape[0] == 1 and mem.shape[0] == 1 and norm_mix_g.shape[0] == 1
    out = _layer(x[0], mem[0], norm_mix_g[0], w_in[0], w_pool[0], pool_scale[0], w_dw[0], b_dw[0], conv_ln_g[0],
                 conv_ln_b[0], w_conv_pw[0], out_norm_a_g[0], out_norm_b_g[0], w_out[0], norm_xattn_g[0],
                 norm_mem_g[0], w_q_mem[0], w_kv_mem[0], w_o_mem[0], norm_ffn_g[0], w_router_group[0],
                 b_router_group[0], w_router_expert[0], b_router_expert[0], w_exp_gate[0], w_exp_up[0],
                 w_exp_down[0], final_norm_g)
    return out[None]
```

```python
import functools

import jax
import jax.numpy as jnp
from jax import lax
from jax.experimental import pallas as pl
from jax.experimental.pallas import tpu as pltpu

F32 = jnp.float32
BF16 = jnp.bfloat16
I32 = jnp.int32

EPS = 1e-6
POOL_WINDOWS = (2, 4, 8, 16)
CONV_KERNEL = 31
XATTN_HEADS = 4
N_GROUPS = 4
EXPERTS_PER_GROUP = 8
N_EXPERTS = N_GROUPS * EXPERTS_PER_GROUP
TOP_K = 2

SUBLANES = 8
LANES = 128
VMEM_LIMIT_BYTES = 56 * 1024 * 1024

POOL_HALO = 16
TIME_PITCH = 2
CONV_HALO = 32
MIXER_TILE = 256
XATTN_TILE = 256
ROUTE_TILE = 512
MOE_BLOCK = 128
COMBINE_TILE = 256
GATHER_PITCH = 24
LOGIT_ROWS = 128
ROUTE_ROWS = 40


def _rms(x, g):
    return x * lax.rsqrt(jnp.mean(x * x, axis=-1, keepdims=True) + EPS) * g


def _resident(shape):
    nd = len(shape)
    return pl.BlockSpec(shape, lambda *_: (0,) * nd, pipeline_mode=pl.Buffered(1))


def _store_token_major(ref, val):
    t, d = val.shape
    chunks = d // LANES
    for c in range(chunks):
        ref[pl.ds(c, t, stride=chunks), :] = val[:, c * LANES:(c + 1) * LANES]


def _load_token_major(ref, t, chunks, pitch):
    return jnp.concatenate([ref[pl.ds(c, t, stride=pitch), :] for c in range(chunks)], axis=-1)


def _mixer_kernel(x_ref, gmix_ref, win_ref, wpool_ref, pscale_ref, wdw_ref, bdw_ref, lng_ref, lnb_ref,
                  wpw_ref, ga_ref, gb_ref, wout_ref, o_ref, ubuf, ybuf, cbuf):
    i = pl.program_id(0)
    tm = x_ref.shape[0]
    pw = ubuf.shape[0] * LANES
    cw = ybuf.shape[0] * LANES
    gdim = pw // len(POOL_WINDOWS)

    @pl.when(i == 0)
    def _():
        ubuf[:, 0:TIME_PITCH * POOL_HALO, :] = jnp.zeros((pw // LANES, TIME_PITCH * POOL_HALO, LANES), F32)
        ybuf[:, 0:TIME_PITCH * CONV_HALO, :] = jnp.zeros((cw // LANES, TIME_PITCH * CONV_HALO, LANES), F32)

    x = x_ref[...]
    h = _rms(x, gmix_ref[...])
    proj = jnp.dot(h.astype(BF16), win_ref[...], preferred_element_type=F32)

    def frames(buf, slab, r, count):
        return buf[slab, pl.ds(TIME_PITCH * r, count, stride=TIME_PITCH), :]

    def put_frames(buf, halo, val):
        for s in range(val.shape[1] // LANES):
            buf[s, pl.ds(TIME_PITCH * halo, tm, stride=TIME_PITCH), :] = val[:, s * LANES:(s + 1) * LANES]

    def keep_halo(buf, halo):
        buf[:, 0:TIME_PITCH * halo, :] = buf[:, TIME_PITCH * tm:TIME_PITCH * (tm + halo), :]

    u = proj[:, :pw]
    put_frames(ubuf, POOL_HALO, u)
    pos = i * tm + lax.broadcasted_iota(I32, (tm, 1), 0)
    zs = []
    for g, w in enumerate(POOL_WINDOWS):
        cols = []
        for s in range(g * gdim // LANES, (g + 1) * gdim // LANES):
            wsum = frames(ubuf, s, POOL_HALO, tm)
            for j in range(1, w):
                wsum = wsum + frames(ubuf, s, POOL_HALO - j, tm)
            cols.append(wsum)
        cnt = jnp.minimum(pos + 1, w).astype(F32)
        d = jnp.concatenate(cols, axis=-1) / cnt - u[:, g * gdim:(g + 1) * gdim]
        zs.append(jnp.dot(d.astype(BF16), wpool_ref[g], preferred_element_type=F32))
    ya = jnp.concatenate(zs, axis=-1) * pscale_ref[...]
    keep_halo(ubuf, POOL_HALO)

    a = proj[:, pw:pw + cw]
    gate = proj[:, pw + cw:pw + 2 * cw]
    put_frames(ybuf, CONV_HALO, a * (1.0 / (1.0 + jnp.exp(-gate))))
    row_chunk = 64
    first = CONV_HALO - (CONV_KERNEL - 1)
    for r0 in range(0, tm, row_chunk):
        for s in range(cw // LANES):
            acc = None
            for k in range(CONV_KERNEL):
                term = frames(ybuf, s, first + r0 + k, row_chunk) * wdw_ref[k:k + 1, s * LANES:(s + 1) * LANES]
                acc = term if acc is None else acc + term
            cbuf[r0:r0 + row_chunk, s * LANES:(s + 1) * LANES] = acc
    keep_halo(ybuf, CONV_HALO)
    c = cbuf[...] + bdw_ref[...]
    mu = jnp.mean(c, axis=-1, keepdims=True)
    var = jnp.mean(jnp.square(c - mu), axis=-1, keepdims=True)
    c = (c - mu) * lax.rsqrt(var + EPS) * lng_ref[...] + lnb_ref[...]
    c = c * (1.0 / (1.0 + jnp.exp(-c)))
    yb = jnp.dot(c.astype(BF16), wpw_ref[...], preferred_element_type=F32)

    y = jnp.concatenate([_rms(ya, ga_ref[...]), _rms(yb, gb_ref[...])], axis=-1)
    o_ref[...] = x + jnp.dot(y.astype(BF16), wout_ref[...], preferred_element_type=F32)


def _mixer(x, g_mix, w_in, w_pool, pool_scale, w_dw, b_dw, ln_g, ln_b, w_pw, g_a, g_b, w_out):
    n, d = x.shape
    pw = w_pool.shape[0] * w_pool.shape[1]
    cw = w_pw.shape[0]
    tm = min(MIXER_TILE, n)
    row = pl.BlockSpec((tm, d), lambda i: (i, 0))
    return pl.pallas_call(
        _mixer_kernel,
        grid=(n // tm,),
        in_specs=[row, _resident(g_mix.shape), _resident(w_in.shape), _resident(w_pool.shape),
                  _resident(pool_scale.shape), _resident(w_dw.shape), _resident(b_dw.shape),
                  _resident(ln_g.shape), _resident(ln_b.shape), _resident(w_pw.shape),
                  _resident(g_a.shape), _resident(g_b.shape), _resident(w_out.shape)],
        out_specs=row,
        out_shape=jax.ShapeDtypeStruct((n, d), F32),
        scratch_shapes=[pltpu.VMEM((pw // LANES, TIME_PITCH * (tm + POOL_HALO), LANES), F32),
                        pltpu.VMEM((cw // LANES, TIME_PITCH * (tm + CONV_HALO), LANES), F32),
                        pltpu.VMEM((tm, cw), F32)],
        compiler_params=pltpu.CompilerParams(dimension_semantics=("arbitrary",),
                                             vmem_limit_bytes=VMEM_LIMIT_BYTES),
        name="mixer",
    )(x, g_mix, w_in, w_pool, pool_scale, w_dw, b_dw, ln_g, ln_b, w_pw, g_a, g_b, w_out)


def _kv_kernel(mem_ref, g_ref, w_ref, o_ref):
    m = _rms(mem_ref[...], g_ref[...])
    o_ref[...] = jnp.dot(m.astype(BF16), w_ref[...].astype(BF16), preferred_element_type=F32).astype(BF16)


def _kv(mem, g_mem, w_kv):
    m, d = mem.shape
    n_out = w_kv.shape[1]
    bn = 512
    return pl.pallas_call(
        _kv_kernel,
        grid=(n_out // bn,),
        in_specs=[_resident((m, d)), _resident(g_mem.shape), pl.BlockSpec((d, bn), lambda j: (0, j))],
        out_specs=pl.BlockSpec((m, bn), lambda j: (0, j)),
        out_shape=jax.ShapeDtypeStruct((m, n_out), BF16),
        compiler_params=pltpu.CompilerParams(dimension_semantics=("arbitrary",),
                                             vmem_limit_bytes=VMEM_LIMIT_BYTES),
        name="kv",
    )(mem, g_mem, w_kv)


def _xattn_kernel(x_ref, gx_ref, wq_ref, kv_ref, wo_ref, gf_ref, wr_ref, o_ref, lg_ref):
    d = x_ref.shape[1]
    hd = d // XATTN_HEADS
    x = x_ref[...]
    h = _rms(x, gx_ref[...])
    q = jnp.dot(h.astype(BF16), wq_ref[...], preferred_element_type=F32)
    outs = []
    for hh in range(XATTN_HEADS):
        qh = q[:, hh * hd:(hh + 1) * hd].astype(BF16)
        kh = kv_ref[:, hh * hd:(hh + 1) * hd]
        vh = kv_ref[:, d + hh * hd:d + (hh + 1) * hd]
        s = lax.dot_general(qh, kh, (((1,), (1,)), ((), ())), preferred_element_type=F32) * (hd ** -0.5)
        e = jnp.exp(s - jnp.max(s, axis=-1, keepdims=True))
        p = e / jnp.sum(e, axis=-1, keepdims=True)
        outs.append(jnp.dot(p.astype(BF16), vh, preferred_element_type=F32))
    o = jnp.concatenate(outs, axis=-1)
    x2 = x + jnp.dot(o.astype(BF16), wo_ref[...], preferred_element_type=F32)
    _store_token_major(o_ref, x2)
    h3 = _rms(x2, gf_ref[...])
    lg_ref[...] = lax.dot_general(wr_ref[...], h3.astype(BF16), (((1,), (1,)), ((), ())),
                                  preferred_element_type=F32)


def _xattn(x, g_x, w_q, kv, w_o, g_ffn, w_router_t):
    n, d = x.shape
    tm = min(XATTN_TILE, n)
    chunks = d // LANES
    row = pl.BlockSpec((tm, d), lambda i: (i, 0))
    return pl.pallas_call(
        _xattn_kernel,
        grid=(n // tm,),
        in_specs=[row, _resident(g_x.shape), _resident(w_q.shape), _resident(kv.shape), _resident(w_o.shape),
                  _resident(g_ffn.shape), _resident(w_router_t.shape)],
        out_specs=[pl.BlockSpec((tm * chunks, LANES), lambda i: (i, 0)),
                   pl.BlockSpec((LOGIT_ROWS, tm), lambda i: (0, i))],
        out_shape=[jax.ShapeDtypeStruct((n * chunks, LANES), F32), jax.ShapeDtypeStruct((LOGIT_ROWS, n), F32)],
        compiler_params=pltpu.CompilerParams(dimension_semantics=("arbitrary",),
                                             vmem_limit_bytes=VMEM_LIMIT_BYTES),
        name="xattn",
    )(x, g_x, w_q, kv, w_o, g_ffn, w_router_t)


def _first_argmax(v, rows):
    m = jnp.max(v, axis=0, keepdims=True)
    idx = jnp.min(jnp.where(v == m, rows.astype(F32), float(v.shape[0])), axis=0, keepdims=True)
    return m, idx.astype(I32)


def _route_kernel(lg_ref, bias_ref, tri_ref, etri_ref, dest_ref, wtok_ref, blk_ref, eg_buf, rank_buf, *, block):
    n = lg_ref.shape[1]
    t = tri_ref.shape[0]
    n_tiles = n // t
    epg = EXPERTS_PER_GROUP
    rows_g = lax.broadcasted_iota(I32, (N_GROUPS, t), 0)
    rows_e = lax.broadcasted_iota(I32, (epg, t), 0)
    rows_all = lax.broadcasted_iota(I32, (N_EXPERTS, t), 0)

    def tile_body(j, carry):
        sl = pl.ds(pl.multiple_of(j * t, t), t)
        le = lg_ref[0:N_EXPERTS, sl] + bias_ref[0:N_EXPERTS, :]
        lgp = lg_ref[N_EXPERTS:N_EXPERTS + N_GROUPS, sl] + bias_ref[N_EXPERTS:N_EXPERTS + N_GROUPS, :]
        eg_ = jnp.exp(lgp - jnp.max(lgp, axis=0, keepdims=True))
        pg = eg_ / jnp.sum(eg_, axis=0, keepdims=True)
        pg_top, g_idx = _first_argmax(pg, rows_g)
        le_sel = jnp.zeros((epg, t), F32)
        for g in range(N_GROUPS):
            le_sel = jnp.where(g_idx == g, le[g * epg:(g + 1) * epg, :], le_sel)
        ee = jnp.exp(le_sel - jnp.max(le_sel, axis=0, keepdims=True))
        pe = ee / jnp.sum(ee, axis=0, keepdims=True)
        p1, i1 = _first_argmax(pe, rows_e)
        p2, i2 = _first_argmax(jnp.where(rows_e == i1, -1.0, pe), rows_e)
        psum = p1 + p2
        w1 = pg_top * p1 / psum
        w2 = pg_top * p2 / psum
        e1 = g_idx * epg + i1
        e2 = g_idx * epg + i2
        oh1 = rows_all == e1
        oh2 = rows_all == e2
        onehot = jnp.where(oh1 | oh2, 1.0, 0.0)
        prefix = carry + jnp.dot(onehot.astype(BF16), tri_ref[...], preferred_element_type=F32)
        r1 = jnp.sum(jnp.where(oh1, prefix, 0.0), axis=0, keepdims=True)
        r2 = jnp.sum(jnp.where(oh2, prefix, 0.0), axis=0, keepdims=True)
        eg_buf[0:1, sl] = e1
        eg_buf[1:2, sl] = e2
        rank_buf[0:1, sl] = r1
        rank_buf[1:2, sl] = r2
        wtok_ref[0:1, sl] = w1
        wtok_ref[1:2, sl] = w2
        return carry + jnp.sum(onehot, axis=1, keepdims=True)

    counts = lax.fori_loop(0, n_tiles, tile_body, jnp.zeros((N_EXPERTS, 1), F32))
    wtok_ref[2:SUBLANES, :] = jnp.zeros((SUBLANES - 2, n), F32)

    nblk = jnp.floor((counts + (block - 1)) * (1.0 / block))
    nblk_b = jnp.broadcast_to(nblk, (N_EXPERTS, LANES)).astype(BF16)
    bstart = jnp.dot(etri_ref[...], nblk_b, preferred_element_type=F32)[:, 0:1]
    bend = bstart + nblk
    pstart = bstart * block

    def dest_body(j, _):
        sl = pl.ds(pl.multiple_of(j * t, t), t)
        for k in range(TOP_K):
            oh = rows_all == eg_buf[k:k + 1, sl]
            base = jnp.sum(jnp.where(oh, pstart, 0.0), axis=0, keepdims=True)
            dest_ref[k:k + 1, sl] = (base + rank_buf[k:k + 1, sl]).astype(I32)
        return 0

    lax.fori_loop(0, n_tiles, dest_body, 0)
    dest_ref[2:SUBLANES, :] = jnp.zeros((SUBLANES - 2, n), I32)

    nbp = blk_ref.shape[1]
    bidx = lax.broadcasted_iota(I32, (N_EXPERTS, nbp), 1).astype(F32)
    be = jnp.sum(jnp.where(bend <= bidx, 1.0, 0.0), axis=0, keepdims=True)
    n_used = jnp.max(bend, axis=0, keepdims=True)
    blk_ref[0:1, :] = be.astype(I32)
    blk_ref[1:2, :] = jnp.broadcast_to(n_used, (1, nbp)).astype(I32)
    blk_ref[2:SUBLANES, :] = jnp.zeros((SUBLANES - 2, nbp), I32)


def _route(logits_t, bias, block, nb):
    n = logits_t.shape[1]
    t = min(ROUTE_TILE, n)
    nbp = pl.cdiv(nb, LANES) * LANES
    assert nbp <= 256, "block counts must stay exactly representable in bf16"
    tri = (lax.broadcasted_iota(I32, (t, t), 0) < lax.broadcasted_iota(I32, (t, t), 1)).astype(BF16)
    etri = (lax.broadcasted_iota(I32, (N_EXPERTS, N_EXPERTS), 1)
            < lax.broadcasted_iota(I32, (N_EXPERTS, N_EXPERTS), 0)).astype(BF16)
    return pl.pallas_call(
        functools.partial(_route_kernel, block=block),
        grid=(1,),
        in_specs=[pl.BlockSpec((ROUTE_ROWS, n), lambda i: (0, 0)), _resident(bias.shape), _resident(tri.shape),
                  _resident(etri.shape)],
        out_specs=[pl.BlockSpec((SUBLANES, n), lambda i: (0, 0)), pl.BlockSpec((SUBLANES, n), lambda i: (0, 0)),
                   pl.BlockSpec((SUBLANES, nbp), lambda i: (0, 0))],
        out_shape=[jax.ShapeDtypeStruct((SUBLANES, n), I32), jax.ShapeDtypeStruct((SUBLANES, n), F32),
                   jax.ShapeDtypeStruct((SUBLANES, nbp), I32)],
        scratch_shapes=[pltpu.VMEM((SUBLANES, n), I32), pltpu.VMEM((SUBLANES, n), F32)],
        compiler_params=pltpu.CompilerParams(dimension_semantics=("arbitrary",),
                                             vmem_limit_bytes=VMEM_LIMIT_BYTES),
        name="route",
    )(logits_t, bias, tri, etri)


def _start_token_gather(tok, src_hbm, dst, sem, row0, chunks):
    pltpu.make_async_copy(src_hbm.at[pl.ds(pl.multiple_of(tok * chunks, chunks), chunks)],
                          dst.at[pl.ds(row0, chunks)], sem).start()


def _gather_tokens(idx_ref, idx0, src_hbm, dst, sem, count, chunks, unrolled):
    if unrolled:
        for r in range(count):
            _start_token_gather(idx_ref[0, 0, idx0 + r], src_hbm, dst, sem, r * GATHER_PITCH, chunks)
    else:
        def body(r, _):
            _start_token_gather(idx_ref[0, 0, idx0 + r], src_hbm, dst, sem,
                                pl.multiple_of(r * GATHER_PITCH, SUBLANES), chunks)
            return 0
        lax.fori_loop(0, count, body, 0, unroll=8)


def _wait_token_gathers(src_hbm, dst, sem, count, chunks):
    pltpu.make_async_copy(src_hbm.at[pl.ds(0, count * chunks)], dst.at[pl.ds(0, count * chunks)], sem).wait()


def _experts_kernel(be_ref, nu_ref, tok_ref, tokn_ref, x_hbm, gf_ref, wg_ref, wu_ref, wd_ref, o_ref,
                    xg, sems, wg16, wu16, wd16):
    b = pl.program_id(0)
    nb = pl.num_programs(0)
    rows = tok_ref.shape[2]
    chunks = o_ref.shape[0] // rows
    slot = b % 2
    n_used = nu_ref[0]

    @pl.when(b == 0)
    def _():
        _gather_tokens(tok_ref, 0, x_hbm, xg.at[0], sems.at[0], rows, chunks, unrolled=False)

    @pl.when(jnp.logical_or(b == 0, be_ref[b] != be_ref[jnp.maximum(b - 1, 0)]))
    def _():
        wg16[...] = wg_ref[...].astype(BF16)
        wu16[...] = wu_ref[...].astype(BF16)
        wd16[...] = wd_ref[...].astype(BF16)

    _wait_token_gathers(x_hbm, xg.at[slot], sems.at[slot], rows, chunks)

    @pl.when(b < n_used)
    def _():
        xb = _load_token_major(xg.at[slot], rows, chunks, GATHER_PITCH)
        _gather_tokens(tokn_ref, 0, x_hbm, xg.at[1 - slot], sems.at[1 - slot], rows, chunks, unrolled=True)
        h = _rms(xb, gf_ref[...]).astype(BF16)
        hg = jnp.dot(h, wg16[...], preferred_element_type=F32)
        hu = jnp.dot(h, wu16[...], preferred_element_type=F32)
        hid = hg * (1.0 / (1.0 + jnp.exp(-hg))) * hu
        _store_token_major(o_ref, jnp.dot(hid.astype(BF16), wd16[...], preferred_element_type=F32))

    @pl.when(b >= n_used)
    def _():
        _gather_tokens(tokn_ref, 0, x_hbm, xg.at[1 - slot], sems.at[1 - slot], rows, chunks, unrolled=False)
        o_ref[...] = jnp.zeros(o_ref.shape, F32)

    @pl.when(b == nb - 1)
    def _():
        _wait_token_gathers(x_hbm, xg.at[1 - slot], sems.at[1 - slot], rows, chunks)


def _experts(block_e, n_used, slot_tok, x2c, g_ffn, w_gate, w_up, w_down, block):
    nb = slot_tok.shape[0]
    d, de = w_gate.shape[1], w_gate.shape[2]
    chunks = d // LANES
    assert chunks <= GATHER_PITCH
    smem_blk = lambda f: pl.BlockSpec((1, 1, block), f, memory_space=pltpu.SMEM)
    grid_spec = pltpu.PrefetchScalarGridSpec(
        num_scalar_prefetch=2,
        grid=(nb,),
        in_specs=[smem_blk(lambda b, be, nu: (b, 0, 0)),
                  smem_blk(lambda b, be, nu: (jnp.minimum(b + 1, nb - 1), 0, 0)),
                  pl.BlockSpec(memory_space=pl.ANY),
                  _resident(g_ffn.shape),
                  pl.BlockSpec((None, d, de), lambda b, be, nu: (be[b], 0, 0)),
                  pl.BlockSpec((None, d, de), lambda b, be, nu: (be[b], 0, 0)),
                  pl.BlockSpec((None, de, d), lambda b, be, nu: (be[b], 0, 0))],
        out_specs=pl.BlockSpec((block * chunks, LANES), lambda b, be, nu: (b, 0)),
        scratch_shapes=[pltpu.VMEM((2, block * GATHER_PITCH, LANES), F32), pltpu.SemaphoreType.DMA((2,)),
                        pltpu.VMEM((d, de), BF16), pltpu.VMEM((d, de), BF16), pltpu.VMEM((de, d), BF16)],
    )
    return pl.pallas_call(
        _experts_kernel,
        grid_spec=grid_spec,
        out_shape=jax.ShapeDtypeStruct((nb * block * chunks, LANES), F32),
        compiler_params=pltpu.CompilerParams(dimension_semantics=("arbitrary",),
                                             vmem_limit_bytes=VMEM_LIMIT_BYTES),
        name="experts",
    )(block_e, n_used, slot_tok, slot_tok, x2c, g_ffn, w_gate, w_up, w_down)


def _combine_kernel(d_ref, dn_ref, x_ref, w_ref, y_hbm, g_ref, o_ref, ybuf, sems):
    i = pl.program_id(0)
    nt = pl.num_programs(0)
    tm, d = o_ref.shape
    chunks = d // LANES
    slot = i % 2

    def start(idx_ref, s):
        for k in range(TOP_K):
            _gather_tokens(idx_ref, k * tm, y_hbm, ybuf.at[s, k], sems.at[s], tm, chunks, unrolled=False)

    def wait(s):
        for k in range(TOP_K):
            _wait_token_gathers(y_hbm, ybuf.at[s, k], sems.at[s], tm, chunks)

    @pl.when(i == 0)
    def _():
        start(d_ref, 0)

    start(dn_ref, 1 - slot)
    wait(slot)
    w = w_ref[...]
    y0 = _load_token_major(ybuf.at[slot, 0], tm, chunks, GATHER_PITCH)
    y1 = _load_token_major(ybuf.at[slot, 1], tm, chunks, GATHER_PITCH)
    y = _load_token_major(x_ref, tm, chunks, chunks) + (y0 * w[:, 0:1] + y1 * w[:, 1:2])
    o_ref[...] = _rms(y, g_ref[...])

    @pl.when(i == nt - 1)
    def _():
        wait(1 - slot)


def _combine(dest_tiles, x2c, w_tok, ysc, g_final):
    d = g_final.shape[1]
    chunks = d // LANES
    n = x2c.shape[0] // chunks
    tm = dest_tiles.shape[2] // TOP_K
    nt = n // tm
    smem_blk = lambda f: pl.BlockSpec((1, 1, TOP_K * tm), f, memory_space=pltpu.SMEM)
    return pl.pallas_call(
        _combine_kernel,
        grid=(nt,),
        in_specs=[smem_blk(lambda i: (i, 0, 0)), smem_blk(lambda i: (jnp.minimum(i + 1, nt - 1), 0, 0)),
                  pl.BlockSpec((tm * chunks, LANES), lambda i: (i, 0)),
                  pl.BlockSpec((tm, SUBLANES), lambda i: (i, 0)), pl.BlockSpec(memory_space=pl.ANY),
                  _resident(g_final.shape)],
        out_specs=pl.BlockSpec((tm, d), lambda i: (i, 0)),
        out_shape=jax.ShapeDtypeStruct((n, d), F32),
        scratch_shapes=[pltpu.VMEM((2, TOP_K, tm * GATHER_PITCH, LANES), F32), pltpu.SemaphoreType.DMA((2,))],
        compiler_params=pltpu.CompilerParams(dimension_semantics=("arbitrary",),
                                             vmem_limit_bytes=VMEM_LIMIT_BYTES),
        name="combine",
    )(dest_tiles, dest_tiles, x2c, w_tok, ysc, g_final)


def _layer(x, mem, norm_mix_g, w_in, w_pool, pool_scale, w_dw, b_dw, conv_ln_g, conv_ln_b, w_conv_pw,
           out_norm_a_g, out_norm_b_g, w_out, norm_xattn_g, norm_mem_g, w_q_mem, w_kv_mem, w_o_mem, norm_ffn_g,
           w_router_group, b_router_group, w_router_expert, b_router_expert, w_exp_gate, w_exp_up, w_exp_down,
           out_gain):
    n, d = x.shape
    r2 = lambda v: v.reshape(1, -1)
    cw = w_conv_pw.shape[0]
    x1 = _mixer(x, r2(norm_mix_g), w_in.astype(BF16), w_pool.astype(BF16), r2(pool_scale),
                w_dw.reshape(CONV_KERNEL, cw), r2(b_dw),
                r2(conv_ln_g), r2(conv_ln_b), w_conv_pw.astype(BF16), r2(out_norm_a_g), r2(out_norm_b_g),
                w_out.astype(BF16))

    kv = _kv(mem, r2(norm_mem_g), w_kv_mem)
    w_router_t = jnp.zeros((LOGIT_ROWS, d), F32)
    w_router_t = w_router_t.at[:N_EXPERTS].set(w_router_expert.T).at[N_EXPERTS:N_EXPERTS + N_GROUPS].set(
        w_router_group.T).astype(BF16)
    x2, logits_t = _xattn(x1, r2(norm_xattn_g), w_q_mem.astype(BF16), kv, w_o_mem.astype(BF16), r2(norm_ffn_g),
                          w_router_t)

    block = MOE_BLOCK
    nb = pl.cdiv(n * TOP_K, block) + N_EXPERTS
    bias = jnp.zeros((ROUTE_ROWS, 1), F32)
    bias = bias.at[:N_EXPERTS, 0].set(b_router_expert).at[N_EXPERTS:N_EXPERTS + N_GROUPS, 0].set(b_router_group)
    dest, w_tok, blk = _route(logits_t, bias, block, nb)
    dest = dest[:TOP_K]
    n_used = blk[1, :1]
    bidx = jnp.arange(nb, dtype=I32)
    block_e = jnp.minimum(blk[0, :nb], N_EXPERTS - 1)
    block_e = jnp.where(bidx < n_used[0], block_e, block_e[jnp.maximum(n_used[0] - 1, 0)])
    tok = jnp.broadcast_to(jnp.arange(n, dtype=I32), (TOP_K, n))
    slot_tok = jnp.zeros((nb * block,), I32).at[dest.reshape(-1)].set(
        tok.reshape(-1), unique_indices=True, mode="promise_in_bounds").reshape(nb, 1, block)

    ys = _experts(block_e, n_used, slot_tok, x2, r2(norm_ffn_g), w_exp_gate, w_exp_up, w_exp_down, block)

    tm = min(COMBINE_TILE, n)
    dest_tiles = dest.reshape(TOP_K, n // tm, tm).transpose(1, 0, 2).reshape(n // tm, 1, TOP_K * tm)
    return _combine(dest_tiles, x2, w_tok.T, ys, r2(out_gain))


def kernel(x, mem, norm_mix_g, w_in, w_pool, pool_scale, w_dw, b_dw, conv_ln_g, conv_ln_b, w_conv_pw, out_norm_a_g,
           out_norm_b_g, w_out, norm_xattn_g, norm_mem_g, w_q_mem, w_kv_mem, w_o_mem, norm_ffn_g, w_router_group,
           b_router_group, w_router_expert, b_router_expert, w_exp_gate, w_exp_up, w_exp_down, final_norm_g):
    assert x.shape[0] == 1 and mem.shape[0] == 1 and norm_mix_g.shape[0] == 1
    out = _layer(x[0], mem[0], norm_mix_g[0], w_in[0], w_pool[0], pool_scale[0], w_dw[0], b_dw[0], conv_ln_g[0],
                 conv_ln_b[0], w_conv_pw[0], out_norm_a_g[0], out_norm_b_g[0], w_out[0], norm_xattn_g[0],
                 norm_mem_g[0], w_q_mem[0], w_kv_mem[0], w_o_mem[0], norm_ffn_g[0], w_router_group[0],
                 b_router_group[0], w_router_expert[0], b_router_expert[0], w_exp_gate[0], w_exp_up[0],
                 w_exp_down[0], final_norm_g)
    return out[None]
```

```python
import functools

import jax
import jax.numpy as jnp
from jax import lax
from jax.experimental import pallas as pl
from jax.experimental.pallas import tpu as pltpu

F32 = jnp.float32
BF16 = jnp.bfloat16
I32 = jnp.int32

EPS = 1e-6
POOL_WINDOWS = (2, 4, 8, 16)
CONV_KERNEL = 31
XATTN_HEADS = 4
N_GROUPS = 4
EXPERTS_PER_GROUP = 8
N_EXPERTS = N_GROUPS * EXPERTS_PER_GROUP
TOP_K = 2

SUBLANES = 8
LANES = 128
VMEM_LIMIT_BYTES = 56 * 1024 * 1024

POOL_HALO = 16
TIME_PITCH = 2
CONV_HALO = 32
MIXER_TILE = 256
XATTN_TILE = 256
ROUTE_TILE = 512
MOE_BLOCK = 256
COMBINE_TILE = 256
LOGIT_ROWS = 128
ROUTE_ROWS = 40


def _rms(x, g):
    return x * lax.rsqrt(jnp.mean(x * x, axis=-1, keepdims=True) + EPS) * g


def _resident(shape):
    nd = len(shape)
    return pl.BlockSpec(shape, lambda *_: (0,) * nd, pipeline_mode=pl.Buffered(1))


def _store_token_major(ref, val):
    t, d = val.shape
    chunks = d // LANES
    for c in range(chunks):
        ref[pl.ds(c, t, stride=chunks), :] = val[:, c * LANES:(c + 1) * LANES]


def _load_token_major(ref, t, chunks, pitch):
    return jnp.concatenate([ref[pl.ds(c, t, stride=pitch), :] for c in range(chunks)], axis=-1)


def _gather_pitch(chunks):
    tiles = pl.cdiv(chunks, SUBLANES)
    return SUBLANES * (tiles + 1 - tiles % 2)


def _mixer_kernel(x_ref, gmix_ref, win_ref, wpool_ref, pscale_ref, wdw_ref, bdw_ref, lng_ref, lnb_ref,
                  wpw_ref, ga_ref, gb_ref, wout_ref, o_ref, ubuf, ybuf, cbuf):
    i = pl.program_id(0)
    tm = x_ref.shape[0]
    pw = ubuf.shape[0] * LANES
    cw = ybuf.shape[0] * LANES
    gdim = pw // len(POOL_WINDOWS)

    @pl.when(i == 0)
    def _():
        ubuf[:, 0:TIME_PITCH * POOL_HALO, :] = jnp.zeros((pw // LANES, TIME_PITCH * POOL_HALO, LANES), F32)
        ybuf[:, 0:TIME_PITCH * CONV_HALO, :] = jnp.zeros((cw // LANES, TIME_PITCH * CONV_HALO, LANES), F32)

    x = x_ref[...]
    h = _rms(x, gmix_ref[...])
    proj = jnp.dot(h.astype(BF16), win_ref[...], preferred_element_type=F32)

    def frames(buf, slab, r, count):
        return buf[slab, pl.ds(TIME_PITCH * r, count, stride=TIME_PITCH), :]

    def put_frames(buf, halo, val):
        for s in range(val.shape[1] // LANES):
            buf[s, pl.ds(TIME_PITCH * halo, tm, stride=TIME_PITCH), :] = val[:, s * LANES:(s + 1) * LANES]

    def keep_halo(buf, halo):
        buf[:, 0:TIME_PITCH * halo, :] = buf[:, TIME_PITCH * tm:TIME_PITCH * (tm + halo), :]

    u = proj[:, :pw]
    put_frames(ubuf, POOL_HALO, u)
    pos = i * tm + lax.broadcasted_iota(I32, (tm, 1), 0)
    zs = []
    for g, w in enumerate(POOL_WINDOWS):
        cols = []
        for s in range(g * gdim // LANES, (g + 1) * gdim // LANES):
            wsum = frames(ubuf, s, POOL_HALO, tm)
            for j in range(1, w):
                wsum = wsum + frames(ubuf, s, POOL_HALO - j, tm)
            cols.append(wsum)
        cnt = jnp.minimum(pos + 1, w).astype(F32)
        d = jnp.concatenate(cols, axis=-1) / cnt - u[:, g * gdim:(g + 1) * gdim]
        zs.append(jnp.dot(d.astype(BF16), wpool_ref[g], preferred_element_type=F32))
    ya = jnp.concatenate(zs, axis=-1) * pscale_ref[...]
    keep_halo(ubuf, POOL_HALO)

    a = proj[:, pw:pw + cw]
    gate = proj[:, pw + cw:pw + 2 * cw]
    put_frames(ybuf, CONV_HALO, a * (1.0 / (1.0 + jnp.exp(-gate))))
    row_chunk = 64
    first = CONV_HALO - (CONV_KERNEL - 1)
    for r0 in range(0, tm, row_chunk):
        for s in range(cw // LANES):
            acc = None
            for k in range(CONV_KERNEL):
                term = frames(ybuf, s, first + r0 + k, row_chunk) * wdw_ref[k:k + 1, s * LANES:(s + 1) * LANES]
                acc = term if acc is None else acc + term
            cbuf[r0:r0 + row_chunk, s * LANES:(s + 1) * LANES] = acc
    keep_halo(ybuf, CONV_HALO)
    c = cbuf[...] + bdw_ref[...]
    mu = jnp.mean(c, axis=-1, keepdims=True)
    var = jnp.mean(jnp.square(c - mu), axis=-1, keepdims=True)
    c = (c - mu) * lax.rsqrt(var + EPS) * lng_ref[...] + lnb_ref[...]
    c = c * (1.0 / (1.0 + jnp.exp(-c)))
    yb = jnp.dot(c.astype(BF16), wpw_ref[...], preferred_element_type=F32)

    y = jnp.concatenate([_rms(ya, ga_ref[...]), _rms(yb, gb_ref[...])], axis=-1)
    o_ref[...] = x + jnp.dot(y.astype(BF16), wout_ref[...], preferred_element_type=F32)


def _mixer(x, g_mix, w_in, w_pool, pool_scale, w_dw, b_dw, ln_g, ln_b, w_pw, g_a, g_b, w_out):
    n, d = x.shape
    pw = w_pool.shape[0] * w_pool.shape[1]
    cw = w_pw.shape[0]
    tm = min(MIXER_TILE, n)
    row = pl.BlockSpec((tm, d), lambda i: (i, 0))
    return pl.pallas_call(
        _mixer_kernel,
        grid=(n // tm,),
        in_specs=[row, _resident(g_mix.shape), _resident(w_in.shape), _resident(w_pool.shape),
                  _resident(pool_scale.shape), _resident(w_dw.shape), _resident(b_dw.shape),
                  _resident(ln_g.shape), _resident(ln_b.shape), _resident(w_pw.shape),
                  _resident(g_a.shape), _resident(g_b.shape), _resident(w_out.shape)],
        out_specs=row,
        out_shape=jax.ShapeDtypeStruct((n, d), F32),
        scratch_shapes=[pltpu.VMEM((pw // LANES, TIME_PITCH * (tm + POOL_HALO), LANES), F32),
                        pltpu.VMEM((cw // LANES, TIME_PITCH * (tm + CONV_HALO), LANES), F32),
                        pltpu.VMEM((tm, cw), F32)],
        compiler_params=pltpu.CompilerParams(dimension_semantics=("arbitrary",),
                                             vmem_limit_bytes=VMEM_LIMIT_BYTES),
        name="mixer",
    )(x, g_mix, w_in, w_pool, pool_scale, w_dw, b_dw, ln_g, ln_b, w_pw, g_a, g_b, w_out)


def _kv_kernel(mem_ref, g_ref, w_ref, o_ref):
    m = _rms(mem_ref[...], g_ref[...])
    o_ref[...] = jnp.dot(m.astype(BF16), w_ref[...].astype(BF16), preferred_element_type=F32).astype(BF16)


def _kv(mem, g_mem, w_kv):
    m, d = mem.shape
    n_out = w_kv.shape[1]
    bn = 512
    return pl.pallas_call(
        _kv_kernel,
        grid=(n_out // bn,),
        in_specs=[_resident((m, d)), _resident(g_mem.shape), pl.BlockSpec((d, bn), lambda j: (0, j))],
        out_specs=pl.BlockSpec((m, bn), lambda j: (0, j)),
        out_shape=jax.ShapeDtypeStruct((m, n_out), BF16),
        compiler_params=pltpu.CompilerParams(dimension_semantics=("arbitrary",),
                                             vmem_limit_bytes=VMEM_LIMIT_BYTES),
        name="kv",
    )(mem, g_mem, w_kv)


def _xattn_kernel(x_ref, gx_ref, wq_ref, kv_ref, wo_ref, gf_ref, wr_ref, o_ref, lg_ref):
    d = x_ref.shape[1]
    hd = d // XATTN_HEADS
    x = x_ref[...]
    h = _rms(x, gx_ref[...])
    q = jnp.dot(h.astype(BF16), wq_ref[...], preferred_element_type=F32)
    outs = []
    for hh in range(XATTN_HEADS):
        qh = q[:, hh * hd:(hh + 1) * hd].astype(BF16)
        kh = kv_ref[:, hh * hd:(hh + 1) * hd]
        vh = kv_ref[:, d + hh * hd:d + (hh + 1) * hd]
        s = lax.dot_general(qh, kh, (((1,), (1,)), ((), ())), preferred_element_type=F32) * (hd ** -0.5)
        e = jnp.exp(s - jnp.max(s, axis=-1, keepdims=True))
        p = e / jnp.sum(e, axis=-1, keepdims=True)
        outs.append(jnp.dot(p.astype(BF16), vh, preferred_element_type=F32))
    o = jnp.concatenate(outs, axis=-1)
    x2 = x + jnp.dot(o.astype(BF16), wo_ref[...], preferred_element_type=F32)
    _store_token_major(o_ref, x2)
    h3 = _rms(x2, gf_ref[...])
    lg_ref[...] = lax.dot_general(wr_ref[...], h3.astype(BF16), (((1,), (1,)), ((), ())),
                                  preferred_element_type=F32)


def _xattn(x, g_x, w_q, kv, w_o, g_ffn, w_router_t):
    n, d = x.shape
    tm = min(XATTN_TILE, n)
    chunks = d // LANES
    row = pl.BlockSpec((tm, d), lambda i: (i, 0))
    return pl.pallas_call(
        _xattn_kernel,
        grid=(n // tm,),
        in_specs=[row, _resident(g_x.shape), _resident(w_q.shape), _resident(kv.shape), _resident(w_o.shape),
                  _resident(g_ffn.shape), _resident(w_router_t.shape)],
        out_specs=[pl.BlockSpec((tm * chunks, LANES), lambda i: (i, 0)),
                   pl.BlockSpec((LOGIT_ROWS, tm), lambda i: (0, i))],
        out_shape=[jax.ShapeDtypeStruct((n * chunks, LANES), F32), jax.ShapeDtypeStruct((LOGIT_ROWS, n), F32)],
        compiler_params=pltpu.CompilerParams(dimension_semantics=("arbitrary",),
                                             vmem_limit_bytes=VMEM_LIMIT_BYTES),
        name="xattn",
    )(x, g_x, w_q, kv, w_o, g_ffn, w_router_t)


def _first_argmax(v, rows):
    m = jnp.max(v, axis=0, keepdims=True)
    idx = jnp.min(jnp.where(v == m, rows.astype(F32), float(v.shape[0])), axis=0, keepdims=True)
    return m, idx.astype(I32)


def _route_kernel(lg_ref, bias_ref, tri_ref, etri_ref, dest_ref, wtok_ref, blk_ref, bst_ref, eg_buf, rank_buf,
                  *, block):
    n = lg_ref.shape[1]
    t = tri_ref.shape[0]
    n_tiles = n // t
    epg = EXPERTS_PER_GROUP
    rows_g = lax.broadcasted_iota(I32, (N_GROUPS, t), 0)
    rows_e = lax.broadcasted_iota(I32, (epg, t), 0)
    rows_all = lax.broadcasted_iota(I32, (N_EXPERTS, t), 0)

    def tile_body(j, carry):
        sl = pl.ds(pl.multiple_of(j * t, t), t)
        le = lg_ref[0:N_EXPERTS, sl] + bias_ref[0:N_EXPERTS, :]
        lgp = lg_ref[N_EXPERTS:N_EXPERTS + N_GROUPS, sl] + bias_ref[N_EXPERTS:N_EXPERTS + N_GROUPS, :]
        eg_ = jnp.exp(lgp - jnp.max(lgp, axis=0, keepdims=True))
        pg = eg_ / jnp.sum(eg_, axis=0, keepdims=True)
        pg_top, g_idx = _first_argmax(pg, rows_g)
        le_sel = jnp.zeros((epg, t), F32)
        for g in range(N_GROUPS):
            le_sel = jnp.where(g_idx == g, le[g * epg:(g + 1) * epg, :], le_sel)
        ee = jnp.exp(le_sel - jnp.max(le_sel, axis=0, keepdims=True))
        pe = ee / jnp.sum(ee, axis=0, keepdims=True)
        p1, i1 = _first_argmax(pe, rows_e)
        p2, i2 = _first_argmax(jnp.where(rows_e == i1, -1.0, pe), rows_e)
        psum = p1 + p2
        w1 = pg_top * p1 / psum
        w2 = pg_top * p2 / psum
        e1 = g_idx * epg + i1
        e2 = g_idx * epg + i2
        oh1 = rows_all == e1
        oh2 = rows_all == e2
        onehot = jnp.where(oh1 | oh2, 1.0, 0.0)
        prefix = carry + jnp.dot(onehot.astype(BF16), tri_ref[...], preferred_element_type=F32)
        r1 = jnp.sum(jnp.where(oh1, prefix, 0.0), axis=0, keepdims=True)
        r2 = jnp.sum(jnp.where(oh2, prefix, 0.0), axis=0, keepdims=True)
        eg_buf[0:1, sl] = e1
        eg_buf[1:2, sl] = e2
        rank_buf[0:1, sl] = r1
        rank_buf[1:2, sl] = r2
        wtok_ref[0:1, sl] = w1
        wtok_ref[1:2, sl] = w2
        return carry + jnp.sum(onehot, axis=1, keepdims=True)

    counts = lax.fori_loop(0, n_tiles, tile_body, jnp.zeros((N_EXPERTS, 1), F32))
    wtok_ref[2:SUBLANES, :] = jnp.zeros((SUBLANES - 2, n), F32)

    nblk = jnp.floor((counts + (block - 1)) * (1.0 / block))
    nblk_b = jnp.broadcast_to(nblk, (N_EXPERTS, LANES)).astype(BF16)
    bstart = jnp.dot(etri_ref[...], nblk_b, preferred_element_type=F32)[:, 0:1]
    bend = bstart + nblk
    pstart = bstart * block

    def dest_body(j, _):
        sl = pl.ds(pl.multiple_of(j * t, t), t)
        for k in range(TOP_K):
            oh = rows_all == eg_buf[k:k + 1, sl]
            base = jnp.sum(jnp.where(oh, pstart, 0.0), axis=0, keepdims=True)
            dest_ref[k:k + 1, sl] = (base + rank_buf[k:k + 1, sl]).astype(I32)
        return 0

    lax.fori_loop(0, n_tiles, dest_body, 0)
    dest_ref[2:SUBLANES, :] = jnp.zeros((SUBLANES - 2, n), I32)

    nbp = blk_ref.shape[1]
    bidx = lax.broadcasted_iota(I32, (N_EXPERTS, nbp), 1).astype(F32)
    be = jnp.sum(jnp.where(bend <= bidx, 1.0, 0.0), axis=0, keepdims=True)
    n_used = jnp.max(bend, axis=0, keepdims=True)
    blk_ref[0:1, :] = be.astype(I32)
    blk_ref[1:2, :] = jnp.broadcast_to(n_used, (1, nbp)).astype(I32)
    blk_ref[2:SUBLANES, :] = jnp.zeros((SUBLANES - 2, nbp), I32)
    bst_ref[...] = jnp.broadcast_to(bstart, bst_ref.shape).astype(I32)


def _route(logits_t, bias, block, nb):
    n = logits_t.shape[1]
    t = min(ROUTE_TILE, n)
    nbp = pl.cdiv(nb, LANES) * LANES
    assert nbp <= 256, "block counts must stay exactly representable in bf16"
    tri = (lax.broadcasted_iota(I32, (t, t), 0) < lax.broadcasted_iota(I32, (t, t), 1)).astype(BF16)
    etri = (lax.broadcasted_iota(I32, (N_EXPERTS, N_EXPERTS), 1)
            < lax.broadcasted_iota(I32, (N_EXPERTS, N_EXPERTS), 0)).astype(BF16)
    return pl.pallas_call(
        functools.partial(_route_kernel, block=block),
        grid=(1,),
        in_specs=[pl.BlockSpec((ROUTE_ROWS, n), lambda i: (0, 0)), _resident(bias.shape), _resident(tri.shape),
                  _resident(etri.shape)],
        out_specs=[pl.BlockSpec((SUBLANES, n), lambda i: (0, 0)), pl.BlockSpec((SUBLANES, n), lambda i: (0, 0)),
                   pl.BlockSpec((SUBLANES, nbp), lambda i: (0, 0)),
                   pl.BlockSpec((N_EXPERTS, LANES), lambda i: (0, 0))],
        out_shape=[jax.ShapeDtypeStruct((SUBLANES, n), I32), jax.ShapeDtypeStruct((SUBLANES, n), F32),
                   jax.ShapeDtypeStruct((SUBLANES, nbp), I32), jax.ShapeDtypeStruct((N_EXPERTS, LANES), I32)],
        scratch_shapes=[pltpu.VMEM((SUBLANES, n), I32), pltpu.VMEM((SUBLANES, n), F32)],
        compiler_params=pltpu.CompilerParams(dimension_semantics=("arbitrary",),
                                             vmem_limit_bytes=VMEM_LIMIT_BYTES),
        name="route",
    )(logits_t, bias, tri, etri)


def _start_token_gather(tok, src_hbm, dst, sem, row0, chunks):
    pltpu.make_async_copy(src_hbm.at[pl.ds(pl.multiple_of(tok * chunks, chunks), chunks)],
                          dst.at[pl.ds(row0, chunks)], sem).start()


def _gather_tokens(token_of, src_hbm, dst, sem, count, chunks, unrolled):
    pitch = _gather_pitch(chunks)
    if unrolled:
        for r in range(count):
            _start_token_gather(token_of(r), src_hbm, dst, sem, r * pitch, chunks)
    else:
        def body(r, _):
            _start_token_gather(token_of(r), src_hbm, dst, sem, pl.multiple_of(r * pitch, SUBLANES), chunks)
            return 0
        lax.fori_loop(0, count, body, 0, unroll=8)


def _wait_token_gathers(src_hbm, dst, sem, count, chunks):
    pltpu.make_async_copy(src_hbm.at[pl.ds(0, count * chunks)], dst.at[pl.ds(0, count * chunks)], sem).wait()


def _experts_kernel(bnd_ref, tok_ref, x_hbm, gf_ref, wg_hbm, wu_hbm, wd_hbm, y_hbm,
                    xg, gsem, obuf, osem, wg32, wu32, wd32, wsem, wg16, wu16, wd16, *, rows, n_blocks):
    e = pl.program_id(0)
    ne = pl.num_programs(0)
    block_rows = obuf.shape[1]
    chunks = block_rows // rows
    pitch = _gather_pitch(chunks)
    n_used = bnd_ref[ne]

    def gather(b, slot, unrolled):
        base = b * rows
        _gather_tokens(lambda r: tok_ref[base + r], x_hbm, xg.at[slot], gsem.at[slot], rows, chunks, unrolled)

    def out_copy(b, slot):
        dst = y_hbm.at[pl.ds(pl.multiple_of(b * block_rows, block_rows), block_rows)]
        return pltpu.make_async_copy(obuf.at[slot], dst, osem.at[slot])

    def weight_copies(expert, slot):
        return [pltpu.make_async_copy(hbm.at[expert], buf.at[slot], wsem.at[slot])
                for hbm, buf in ((wg_hbm, wg32), (wu_hbm, wu32), (wd_hbm, wd32))]

    @pl.when(e == 0)
    def _():
        for cp in weight_copies(0, 0):
            cp.start(priority=1)
        gather(0, 0, unrolled=False)

    @pl.when(e + 1 < ne)
    def _():
        for cp in weight_copies(e + 1, (e + 1) % 2):
            cp.start(priority=1)

    for cp in weight_copies(e, e % 2):
        cp.wait()

    @pl.when(bnd_ref[e + 1] > bnd_ref[e])
    def _():
        wg16[...] = wg32[e % 2].astype(BF16)
        wu16[...] = wu32[e % 2].astype(BF16)
        wd16[...] = wd32[e % 2].astype(BF16)

    def block_body(b, carry):
        slot = b % 2

        @pl.when(b >= 2)
        def _():
            out_copy(b - 2, slot).wait()

        _wait_token_gathers(x_hbm, xg.at[slot], gsem.at[slot], rows, chunks)
        xb = _load_token_major(xg.at[slot], rows, chunks, pitch)
        gather(jnp.minimum(b + 1, n_blocks - 1), 1 - slot, unrolled=True)
        h = _rms(xb, gf_ref[...]).astype(BF16)
        hg = jnp.dot(h, wg16[...], preferred_element_type=F32)
        hu = jnp.dot(h, wu16[...], preferred_element_type=F32)
        hid = hg * (1.0 / (1.0 + jnp.exp(-hg))) * hu
        _store_token_major(obuf.at[slot], jnp.dot(hid.astype(BF16), wd16[...], preferred_element_type=F32))
        out_copy(b, slot).start()
        return carry

    lax.fori_loop(bnd_ref[e], bnd_ref[e + 1], block_body, 0)

    @pl.when(e == ne - 1)
    def _():
        _wait_token_gathers(x_hbm, xg.at[n_used % 2], gsem.at[n_used % 2], rows, chunks)

        @pl.when(n_used >= 2)
        def _():
            out_copy(n_used - 2, n_used % 2).wait()

        out_copy(n_used - 1, (n_used - 1) % 2).wait()
        obuf[0] = jnp.zeros(obuf.shape[1:], F32)

        def fill(b, carry):
            out_copy(b, 0).start()
            return carry

        def drain(b, carry):
            out_copy(b, 0).wait()
            return carry

        lax.fori_loop(n_used, n_blocks, fill, 0)
        lax.fori_loop(n_used, n_blocks, drain, 0)


def _experts(bounds, slot_tok, x2c, g_ffn, w_gate, w_up, w_down, block):
    n_blocks = slot_tok.shape[0] // block
    ne, d, de = w_gate.shape
    chunks = d // LANES
    grid_spec = pltpu.PrefetchScalarGridSpec(
        num_scalar_prefetch=2,
        grid=(ne,),
        in_specs=[pl.BlockSpec(memory_space=pl.ANY),
                  _resident(g_ffn.shape),
                  pl.BlockSpec(memory_space=pl.ANY),
                  pl.BlockSpec(memory_space=pl.ANY),
                  pl.BlockSpec(memory_space=pl.ANY)],
        out_specs=pl.BlockSpec(memory_space=pl.ANY),
        scratch_shapes=[pltpu.VMEM((2, block * _gather_pitch(chunks), LANES), F32), pltpu.SemaphoreType.DMA((2,)),
                        pltpu.VMEM((2, block * chunks, LANES), F32), pltpu.SemaphoreType.DMA((2,)),
                        pltpu.VMEM((2, d, de), F32), pltpu.VMEM((2, d, de), F32), pltpu.VMEM((2, de, d), F32),
                        pltpu.SemaphoreType.DMA((2,)),
                        pltpu.VMEM((d, de), BF16), pltpu.VMEM((d, de), BF16), pltpu.VMEM((de, d), BF16)],
    )
    return pl.pallas_call(
        functools.partial(_experts_kernel, rows=block, n_blocks=n_blocks),
        grid_spec=grid_spec,
        out_shape=jax.ShapeDtypeStruct((n_blocks * block * chunks, LANES), F32),
        compiler_params=pltpu.CompilerParams(dimension_semantics=("arbitrary",),
                                             vmem_limit_bytes=VMEM_LIMIT_BYTES),
        name="experts",
    )(bounds, slot_tok, x2c, g_ffn, w_gate, w_up, w_down)


def _combine_kernel(d_ref, dn_ref, x_ref, w_ref, y_hbm, g_ref, o_ref, ybuf, sems):
    i = pl.program_id(0)
    nt = pl.num_programs(0)
    tm, d = o_ref.shape
    chunks = d // LANES
    slot = i % 2

    def expert_rows(k):
        return _load_token_major(ybuf.at[slot, k], tm, chunks, _gather_pitch(chunks))

    def start(idx_ref, s):
        for k in range(TOP_K):
            _gather_tokens(lambda r, k=k: idx_ref[0, 0, k * tm + r], y_hbm, ybuf.at[s, k], sems.at[s], tm, chunks,
                           unrolled=False)

    def wait(s):
        for k in range(TOP_K):
            _wait_token_gathers(y_hbm, ybuf.at[s, k], sems.at[s], tm, chunks)

    @pl.when(i == 0)
    def _():
        start(d_ref, 0)

    start(dn_ref, 1 - slot)
    wait(slot)
    w = w_ref[...]
    y = _load_token_major(x_ref, tm, chunks, chunks) + (expert_rows(0) * w[:, 0:1] + expert_rows(1) * w[:, 1:2])
    o_ref[...] = _rms(y, g_ref[...])

    @pl.when(i == nt - 1)
    def _():
        wait(1 - slot)


def _combine(dest_tiles, x2c, w_tok, ysc, g_final):
    d = g_final.shape[1]
    chunks = d // LANES
    n = x2c.shape[0] // chunks
    tm = dest_tiles.shape[2] // TOP_K
    nt = n // tm
    smem_blk = lambda f: pl.BlockSpec((1, 1, TOP_K * tm), f, memory_space=pltpu.SMEM)
    return pl.pallas_call(
        _combine_kernel,
        grid=(nt,),
        in_specs=[smem_blk(lambda i: (i, 0, 0)), smem_blk(lambda i: (jnp.minimum(i + 1, nt - 1), 0, 0)),
                  pl.BlockSpec((tm * chunks, LANES), lambda i: (i, 0)),
                  pl.BlockSpec((tm, SUBLANES), lambda i: (i, 0)), pl.BlockSpec(memory_space=pl.ANY),
                  _resident(g_final.shape)],
        out_specs=pl.BlockSpec((tm, d), lambda i: (i, 0)),
        out_shape=jax.ShapeDtypeStruct((n, d), F32),
        scratch_shapes=[pltpu.VMEM((2, TOP_K, tm * _gather_pitch(chunks), LANES), F32),
                        pltpu.SemaphoreType.DMA((2,))],
        compiler_params=pltpu.CompilerParams(dimension_semantics=("arbitrary",),
                                             vmem_limit_bytes=VMEM_LIMIT_BYTES),
        name="combine",
    )(dest_tiles, dest_tiles, x2c, w_tok, ysc, g_final)


def _layer(x, mem, norm_mix_g, w_in, w_pool, pool_scale, w_dw, b_dw, conv_ln_g, conv_ln_b, w_conv_pw,
           out_norm_a_g, out_norm_b_g, w_out, norm_xattn_g, norm_mem_g, w_q_mem, w_kv_mem, w_o_mem, norm_ffn_g,
           w_router_group, b_router_group, w_router_expert, b_router_expert, w_exp_gate, w_exp_up, w_exp_down,
           out_gain):
    n, d = x.shape
    r2 = lambda v: v.reshape(1, -1)
    cw = w_conv_pw.shape[0]
    x1 = _mixer(x, r2(norm_mix_g), w_in.astype(BF16), w_pool.astype(BF16), r2(pool_scale),
                w_dw.reshape(CONV_KERNEL, cw), r2(b_dw),
                r2(conv_ln_g), r2(conv_ln_b), w_conv_pw.astype(BF16), r2(out_norm_a_g), r2(out_norm_b_g),
                w_out.astype(BF16))

    kv = _kv(mem, r2(norm_mem_g), w_kv_mem)
    w_router_t = jnp.zeros((LOGIT_ROWS, d), F32)
    w_router_t = w_router_t.at[:N_EXPERTS].set(w_router_expert.T).at[N_EXPERTS:N_EXPERTS + N_GROUPS].set(
        w_router_group.T).astype(BF16)
    x2, logits_t = _xattn(x1, r2(norm_xattn_g), w_q_mem.astype(BF16), kv, w_o_mem.astype(BF16), r2(norm_ffn_g),
                          w_router_t)

    block = MOE_BLOCK
    nb = pl.cdiv(n * TOP_K, block) + N_EXPERTS
    bias = jnp.zeros((ROUTE_ROWS, 1), F32)
    bias = bias.at[:N_EXPERTS, 0].set(b_router_expert).at[N_EXPERTS:N_EXPERTS + N_GROUPS, 0].set(b_router_group)
    dest, w_tok, blk, bst = _route(logits_t, bias, block, nb)
    dest = dest[:TOP_K]
    bounds = jnp.concatenate([bst[:, 0], blk[1, :1]])
    tok = jnp.broadcast_to(jnp.arange(n, dtype=I32), (TOP_K, n))
    slot_tok = (jnp.arange(nb * block, dtype=I32) % n).at[dest.reshape(-1)].set(
        tok.reshape(-1), unique_indices=True, mode="promise_in_bounds")

    ys = _experts(bounds, slot_tok, x2, r2(norm_ffn_g), w_exp_gate, w_exp_up, w_exp_down, block)

    tm = min(COMBINE_TILE, n)
    dest_tiles = dest.reshape(TOP_K, n // tm, tm).transpose(1, 0, 2).reshape(n // tm, 1, TOP_K * tm)
    return _combine(dest_tiles, x2, w_tok.T, ys, r2(out_gain))


def kernel(x, mem, norm_mix_g, w_in, w_pool, pool_scale, w_dw, b_dw, conv_ln_g, conv_ln_b, w_conv_pw, out_norm_a_g,
           out_norm_b_g, w_out, norm_xattn_g, norm_mem_g, w_q_mem, w_kv_mem, w_o_mem, norm_ffn_g, w_router_group,
           b_router_group, w_router_expert, b_router_expert, w_exp_gate, w_exp_up, w_exp_down, final_norm_g):
    assert x.shape[0] == 1 and mem.shape[0] == 1 and norm_mix_g.shape[0] == 1
    out = _layer(x[0], mem[0], norm_mix_g[0], w_in[0], w_pool[0], pool_scale[0], w_dw[0], b_dw[0], conv_ln_g[0],
                 conv_ln_b[0], w_conv_pw[0], out_norm_a_g[0], out_norm_b_g[0], w_out[0], norm_xattn_g[0],
                 norm_mem_g[0], w_q_mem[0], w_kv_mem[0], w_o_mem[0], norm_ffn_g[0], w_router_group[0],
                 b_router_group[0], w_router_expert[0], b_router_expert[0], w_exp_gate[0], w_exp_up[0],
                 w_exp_down[0], final_norm_g)
    return out[None]
```

```python
import functools

import jax
import jax.numpy as jnp
from jax import lax
from jax.experimental import pallas as pl
from jax.experimental.pallas import tpu as pltpu

F32 = jnp.float32
BF16 = jnp.bfloat16
I32 = jnp.int32

EPS = 1e-6
POOL_WINDOWS = (2, 4, 8, 16)
CONV_KERNEL = 31
XATTN_HEADS = 4
N_GROUPS = 4
EXPERTS_PER_GROUP = 8
N_EXPERTS = N_GROUPS * EXPERTS_PER_GROUP
TOP_K = 2

SUBLANES = 8
LANES = 128
VMEM_LIMIT_BYTES = 56 * 1024 * 1024

POOL_HALO = 16
TIME_PITCH = 2
CONV_HALO = 32
MIXER_TILE = 256
MIXER_SUB = 256
XATTN_TILE = 512
ROUTE_TILE = 512
MOE_BLOCK = 256
COMBINE_TILE = 256
LOGIT_ROWS = 128
ROUTE_ROWS = 40


def _rms(x, g):
    return x * lax.rsqrt(jnp.mean(x * x, axis=-1, keepdims=True) + EPS) * g


def _resident(shape):
    nd = len(shape)
    return pl.BlockSpec(shape, lambda *_: (0,) * nd, pipeline_mode=pl.Buffered(1))


def _store_token_major(ref, val):
    t, d = val.shape
    chunks = d // LANES
    for c in range(chunks):
        ref[pl.ds(c, t, stride=chunks), :] = val[:, c * LANES:(c + 1) * LANES]


def _load_token_major(ref, t, chunks, pitch):
    return jnp.concatenate([ref[pl.ds(c, t, stride=pitch), :] for c in range(chunks)], axis=-1)


def _gather_pitch(chunks):
    tiles = pl.cdiv(chunks, SUBLANES)
    return SUBLANES * (tiles + 1 - tiles % 2)


def _mixer_kernel(x_ref, gmix_ref, win_ref, wpool_ref, pscale_ref, wdw_ref, bdw_ref, lng_ref, lnb_ref,
                  wpw_ref, ga_ref, gb_ref, wout_ref, o_ref, ubuf, ybuf, cbuf):
    i = pl.program_id(0)
    tm = x_ref.shape[0]
    pw = ubuf.shape[0] * LANES
    cw = ybuf.shape[0] * LANES
    gdim = pw // len(POOL_WINDOWS)

    @pl.when(i == 0)
    def _():
        ubuf[:, 0:TIME_PITCH * POOL_HALO, :] = jnp.zeros((pw // LANES, TIME_PITCH * POOL_HALO, LANES), F32)
        ybuf[:, 0:TIME_PITCH * CONV_HALO, :] = jnp.zeros((cw // LANES, TIME_PITCH * CONV_HALO, LANES), F32)

    def frames(buf, slab, r, count):
        return buf[slab, pl.ds(TIME_PITCH * r, count, stride=TIME_PITCH), :]

    def put_frames(buf, r, val):
        for s in range(val.shape[1] // LANES):
            buf[s, pl.ds(TIME_PITCH * r, val.shape[0], stride=TIME_PITCH), :] = val[:, s * LANES:(s + 1) * LANES]

    def keep_halo(buf, halo):
        buf[:, 0:TIME_PITCH * halo, :] = buf[:, TIME_PITCH * tm:TIME_PITCH * (tm + halo), :]

    sub = min(MIXER_SUB, tm)
    first = CONV_HALO - (CONV_KERNEL - 1)
    row_chunk = 64
    for t0 in range(0, tm, sub):
        x = x_ref[t0:t0 + sub, :]
        h = _rms(x, gmix_ref[...])
        proj = jnp.dot(h.astype(BF16), win_ref[...], preferred_element_type=F32)

        u = proj[:, :pw]
        put_frames(ubuf, POOL_HALO + t0, u)
        pos = i * tm + t0 + lax.broadcasted_iota(I32, (sub, 1), 0)
        zs = []
        for g, w in enumerate(POOL_WINDOWS):
            cols = []
            for s in range(g * gdim // LANES, (g + 1) * gdim // LANES):
                wsum = frames(ubuf, s, POOL_HALO + t0, sub)
                for j in range(1, w):
                    wsum = wsum + frames(ubuf, s, POOL_HALO + t0 - j, sub)
                cols.append(wsum)
            cnt = jnp.minimum(pos + 1, w).astype(F32)
            d = jnp.concatenate(cols, axis=-1) / cnt - u[:, g * gdim:(g + 1) * gdim]
            zs.append(jnp.dot(d.astype(BF16), wpool_ref[g], preferred_element_type=F32))
        ya = jnp.concatenate(zs, axis=-1) * pscale_ref[...]

        a = proj[:, pw:pw + cw]
        gate = proj[:, pw + cw:pw + 2 * cw]
        put_frames(ybuf, CONV_HALO + t0, a * (1.0 / (1.0 + jnp.exp(-gate))))
        for r0 in range(t0, t0 + sub, row_chunk):
            for s in range(cw // LANES):
                acc = None
                for k in range(CONV_KERNEL):
                    term = frames(ybuf, s, first + r0 + k, row_chunk) * wdw_ref[k:k + 1, s * LANES:(s + 1) * LANES]
                    acc = term if acc is None else acc + term
                cbuf[r0:r0 + row_chunk, s * LANES:(s + 1) * LANES] = acc
        c = cbuf[t0:t0 + sub, :] + bdw_ref[...]
        mu = jnp.mean(c, axis=-1, keepdims=True)
        var = jnp.mean(jnp.square(c - mu), axis=-1, keepdims=True)
        c = (c - mu) * lax.rsqrt(var + EPS) * lng_ref[...] + lnb_ref[...]
        c = c * (1.0 / (1.0 + jnp.exp(-c)))
        yb = jnp.dot(c.astype(BF16), wpw_ref[...], preferred_element_type=F32)

        y = jnp.concatenate([_rms(ya, ga_ref[...]), _rms(yb, gb_ref[...])], axis=-1)
        o_ref[t0:t0 + sub, :] = x + jnp.dot(y.astype(BF16), wout_ref[...], preferred_element_type=F32)
    keep_halo(ubuf, POOL_HALO)
    keep_halo(ybuf, CONV_HALO)


def _mixer(x, g_mix, w_in, w_pool, pool_scale, w_dw, b_dw, ln_g, ln_b, w_pw, g_a, g_b, w_out):
    n, d = x.shape
    pw = w_pool.shape[0] * w_pool.shape[1]
    cw = w_pw.shape[0]
    tm = min(MIXER_TILE, n)
    row = pl.BlockSpec((tm, d), lambda i: (i, 0))
    return pl.pallas_call(
        _mixer_kernel,
        grid=(n // tm,),
        in_specs=[row, _resident(g_mix.shape), _resident(w_in.shape), _resident(w_pool.shape),
                  _resident(pool_scale.shape), _resident(w_dw.shape), _resident(b_dw.shape),
                  _resident(ln_g.shape), _resident(ln_b.shape), _resident(w_pw.shape),
                  _resident(g_a.shape), _resident(g_b.shape), _resident(w_out.shape)],
        out_specs=row,
        out_shape=jax.ShapeDtypeStruct((n, d), F32),
        scratch_shapes=[pltpu.VMEM((pw // LANES, TIME_PITCH * (tm + POOL_HALO), LANES), F32),
                        pltpu.VMEM((cw // LANES, TIME_PITCH * (tm + CONV_HALO), LANES), F32),
                        pltpu.VMEM((tm, cw), F32)],
        compiler_params=pltpu.CompilerParams(dimension_semantics=("arbitrary",),
                                             vmem_limit_bytes=VMEM_LIMIT_BYTES),
        name="mixer",
    )(x, g_mix, w_in, w_pool, pool_scale, w_dw, b_dw, ln_g, ln_b, w_pw, g_a, g_b, w_out)


def _kv_kernel(mem_ref, g_ref, w_ref, o_ref):
    m = _rms(mem_ref[...], g_ref[...])
    o_ref[...] = jnp.dot(m.astype(BF16), w_ref[...].astype(BF16), preferred_element_type=F32).astype(BF16)


def _kv(mem, g_mem, w_kv):
    m, d = mem.shape
    n_out = w_kv.shape[1]
    bn = 512
    return pl.pallas_call(
        _kv_kernel,
        grid=(n_out // bn,),
        in_specs=[_resident((m, d)), _resident(g_mem.shape), pl.BlockSpec((d, bn), lambda j: (0, j))],
        out_specs=pl.BlockSpec((m, bn), lambda j: (0, j)),
        out_shape=jax.ShapeDtypeStruct((m, n_out), BF16),
        compiler_params=pltpu.CompilerParams(dimension_semantics=("arbitrary",),
                                             vmem_limit_bytes=VMEM_LIMIT_BYTES),
        name="kv",
    )(mem, g_mem, w_kv)


def _fold_kernel(wq_ref, wo_ref, kv_ref, a_ref, b_ref):
    hd = wq_ref.shape[1]
    d = wq_ref.shape[0]
    hh = pl.program_id(0)
    col = pl.multiple_of(hh * hd, hd)
    kh = kv_ref[:, pl.ds(col, hd)]
    vh = kv_ref[:, pl.ds(pl.multiple_of(d + hh * hd, hd), hd)]
    a_ref[...] = lax.dot_general(wq_ref[...].astype(BF16), kh, (((1,), (1,)), ((), ())),
                                 preferred_element_type=F32).astype(BF16)
    b_ref[...] = jnp.dot(vh, wo_ref[...].astype(BF16), preferred_element_type=F32).astype(BF16)


def _fold_memory(w_q, w_o, kv):
    d = w_q.shape[0]
    m = kv.shape[0]
    hd = d // XATTN_HEADS
    return pl.pallas_call(
        _fold_kernel,
        grid=(XATTN_HEADS,),
        in_specs=[pl.BlockSpec((d, hd), lambda h: (0, h)), pl.BlockSpec((hd, d), lambda h: (h, 0)),
                  _resident(kv.shape)],
        out_specs=[pl.BlockSpec((d, m), lambda h: (0, h)), pl.BlockSpec((m, d), lambda h: (h, 0))],
        out_shape=[jax.ShapeDtypeStruct((d, XATTN_HEADS * m), BF16), jax.ShapeDtypeStruct((XATTN_HEADS * m, d), BF16)],
        compiler_params=pltpu.CompilerParams(dimension_semantics=("arbitrary",),
                                             vmem_limit_bytes=VMEM_LIMIT_BYTES),
        name="fold",
    )(w_q, w_o, kv)


def _xattn_kernel(x_ref, gx_ref, a_ref, b_ref, gf_ref, wr_ref, o_ref, lg_ref):
    d = x_ref.shape[1]
    hd = d // XATTN_HEADS
    m = a_ref.shape[1] // XATTN_HEADS
    x = x_ref[...]
    h = _rms(x, gx_ref[...])
    s_all = jnp.dot(h.astype(BF16), a_ref[...], preferred_element_type=F32)
    ps = []
    for hh in range(XATTN_HEADS):
        s = s_all[:, hh * m:(hh + 1) * m] * (hd ** -0.5)
        e = jnp.exp(s - jnp.max(s, axis=-1, keepdims=True))
        ps.append((e / jnp.sum(e, axis=-1, keepdims=True)).astype(BF16))
    x2 = x + jnp.dot(jnp.concatenate(ps, axis=-1), b_ref[...], preferred_element_type=F32)
    _store_token_major(o_ref, x2)
    h3 = _rms(x2, gf_ref[...])
    lg_ref[...] = lax.dot_general(wr_ref[...], h3.astype(BF16), (((1,), (1,)), ((), ())),
                                  preferred_element_type=F32)


def _xattn(x, g_x, qk, vo, g_ffn, w_router_t):
    n, d = x.shape
    tm = min(XATTN_TILE, n)
    chunks = d // LANES
    row = pl.BlockSpec((tm, d), lambda i: (i, 0))
    return pl.pallas_call(
        _xattn_kernel,
        grid=(n // tm,),
        in_specs=[row, _resident(g_x.shape), _resident(qk.shape), _resident(vo.shape),
                  _resident(g_ffn.shape), _resident(w_router_t.shape)],
        out_specs=[pl.BlockSpec((tm * chunks, LANES), lambda i: (i, 0)),
                   pl.BlockSpec((LOGIT_ROWS, tm), lambda i: (0, i))],
        out_shape=[jax.ShapeDtypeStruct((n * chunks, LANES), F32), jax.ShapeDtypeStruct((LOGIT_ROWS, n), F32)],
        compiler_params=pltpu.CompilerParams(dimension_semantics=("arbitrary",),
                                             vmem_limit_bytes=VMEM_LIMIT_BYTES),
        name="xattn",
    )(x, g_x, qk, vo, g_ffn, w_router_t)


def _first_argmax(v, rows):
    m = jnp.max(v, axis=0, keepdims=True)
    idx = jnp.min(jnp.where(v == m, rows.astype(F32), float(v.shape[0])), axis=0, keepdims=True)
    return m, idx.astype(I32)


def _route_kernel(lg_ref, bias_ref, tri_ref, etri_ref, dest_ref, wtok_ref, blk_ref, bst_ref, eg_buf, rank_buf,
                  *, block):
    n = lg_ref.shape[1]
    t = tri_ref.shape[0]
    n_tiles = n // t
    epg = EXPERTS_PER_GROUP
    rows_g = lax.broadcasted_iota(I32, (N_GROUPS, t), 0)
    rows_e = lax.broadcasted_iota(I32, (epg, t), 0)
    rows_all = lax.broadcasted_iota(I32, (N_EXPERTS, t), 0)

    def tile_body(j, carry):
        sl = pl.ds(pl.multiple_of(j * t, t), t)
        le = lg_ref[0:N_EXPERTS, sl] + bias_ref[0:N_EXPERTS, :]
        lgp = lg_ref[N_EXPERTS:N_EXPERTS + N_GROUPS, sl] + bias_ref[N_EXPERTS:N_EXPERTS + N_GROUPS, :]
        eg_ = jnp.exp(lgp - jnp.max(lgp, axis=0, keepdims=True))
        pg = eg_ / jnp.sum(eg_, axis=0, keepdims=True)
        pg_top, g_idx = _first_argmax(pg, rows_g)
        le_sel = jnp.zeros((epg, t), F32)
        for g in range(N_GROUPS):
            le_sel = jnp.where(g_idx == g, le[g * epg:(g + 1) * epg, :], le_sel)
        ee = jnp.exp(le_sel - jnp.max(le_sel, axis=0, keepdims=True))
        pe = ee / jnp.sum(ee, axis=0, keepdims=True)
        p1, i1 = _first_argmax(pe, rows_e)
        p2, i2 = _first_argmax(jnp.where(rows_e == i1, -1.0, pe), rows_e)
        psum = p1 + p2
        w1 = pg_top * p1 / psum
        w2 = pg_top * p2 / psum
        e1 = g_idx * epg + i1
        e2 = g_idx * epg + i2
        oh1 = rows_all == e1
        oh2 = rows_all == e2
        onehot = jnp.where(oh1 | oh2, 1.0, 0.0)
        prefix = carry + jnp.dot(onehot.astype(BF16), tri_ref[...], preferred_element_type=F32)
        r1 = jnp.sum(jnp.where(oh1, prefix, 0.0), axis=0, keepdims=True)
        r2 = jnp.sum(jnp.where(oh2, prefix, 0.0), axis=0, keepdims=True)
        eg_buf[0:1, sl] = e1
        eg_buf[1:2, sl] = e2
        rank_buf[0:1, sl] = r1
        rank_buf[1:2, sl] = r2
        wtok_ref[0:1, sl] = w1
        wtok_ref[1:2, sl] = w2
        return carry + jnp.sum(onehot, axis=1, keepdims=True)

    counts = lax.fori_loop(0, n_tiles, tile_body, jnp.zeros((N_EXPERTS, 1), F32))
    wtok_ref[2:SUBLANES, :] = jnp.zeros((SUBLANES - 2, n), F32)

    nblk = jnp.floor((counts + (block - 1)) * (1.0 / block))
    nblk_b = jnp.broadcast_to(nblk, (N_EXPERTS, LANES)).astype(BF16)
    bstart = jnp.dot(etri_ref[...], nblk_b, preferred_element_type=F32)[:, 0:1]
    bend = bstart + nblk
    pstart = bstart * block

    def dest_body(j, _):
        sl = pl.ds(pl.multiple_of(j * t, t), t)
        for k in range(TOP_K):
            oh = rows_all == eg_buf[k:k + 1, sl]
            base = jnp.sum(jnp.where(oh, pstart, 0.0), axis=0, keepdims=True)
            dest_ref[k:k + 1, sl] = (base + rank_buf[k:k + 1, sl]).astype(I32)
        return 0

    lax.fori_loop(0, n_tiles, dest_body, 0)
    dest_ref[2:SUBLANES, :] = jnp.zeros((SUBLANES - 2, n), I32)

    nbp = blk_ref.shape[1]
    bidx = lax.broadcasted_iota(I32, (N_EXPERTS, nbp), 1).astype(F32)
    be = jnp.sum(jnp.where(bend <= bidx, 1.0, 0.0), axis=0, keepdims=True)
    n_used = jnp.max(bend, axis=0, keepdims=True)
    blk_ref[0:1, :] = be.astype(I32)
    blk_ref[1:2, :] = jnp.broadcast_to(n_used, (1, nbp)).astype(I32)
    blk_ref[2:SUBLANES, :] = jnp.zeros((SUBLANES - 2, nbp), I32)
    bst_ref[...] = jnp.broadcast_to(bstart, bst_ref.shape).astype(I32)


def _route(logits_t, bias, block, nb):
    n = logits_t.shape[1]
    t = min(ROUTE_TILE, n)
    nbp = pl.cdiv(nb, LANES) * LANES
    assert nbp <= 256, "block counts must stay exactly representable in bf16"
    tri = (lax.broadcasted_iota(I32, (t, t), 0) < lax.broadcasted_iota(I32, (t, t), 1)).astype(BF16)
    etri = (lax.broadcasted_iota(I32, (N_EXPERTS, N_EXPERTS), 1)
            < lax.broadcasted_iota(I32, (N_EXPERTS, N_EXPERTS), 0)).astype(BF16)
    return pl.pallas_call(
        functools.partial(_route_kernel, block=block),
        grid=(1,),
        in_specs=[pl.BlockSpec((ROUTE_ROWS, n), lambda i: (0, 0)), _resident(bias.shape), _resident(tri.shape),
                  _resident(etri.shape)],
        out_specs=[pl.BlockSpec((SUBLANES, n), lambda i: (0, 0)), pl.BlockSpec((SUBLANES, n), lambda i: (0, 0)),
                   pl.BlockSpec((SUBLANES, nbp), lambda i: (0, 0)),
                   pl.BlockSpec((N_EXPERTS, LANES), lambda i: (0, 0))],
        out_shape=[jax.ShapeDtypeStruct((SUBLANES, n), I32), jax.ShapeDtypeStruct((SUBLANES, n), F32),
                   jax.ShapeDtypeStruct((SUBLANES, nbp), I32), jax.ShapeDtypeStruct((N_EXPERTS, LANES), I32)],
        scratch_shapes=[pltpu.VMEM((SUBLANES, n), I32), pltpu.VMEM((SUBLANES, n), F32)],
        compiler_params=pltpu.CompilerParams(dimension_semantics=("arbitrary",),
                                             vmem_limit_bytes=VMEM_LIMIT_BYTES),
        name="route",
    )(logits_t, bias, tri, etri)


def _start_token_gather(tok, src_hbm, dst, sem, row0, chunks):
    pltpu.make_async_copy(src_hbm.at[pl.ds(pl.multiple_of(tok * chunks, chunks), chunks)],
                          dst.at[pl.ds(row0, chunks)], sem).start()


def _gather_tokens(token_of, src_hbm, dst, sem, count, chunks, unrolled):
    pitch = _gather_pitch(chunks)
    if unrolled:
        for r in range(count):
            _start_token_gather(token_of(r), src_hbm, dst, sem, r * pitch, chunks)
    else:
        def body(r, _):
            _start_token_gather(token_of(r), src_hbm, dst, sem, pl.multiple_of(r * pitch, SUBLANES), chunks)
            return 0
        lax.fori_loop(0, count, body, 0, unroll=8)


def _wait_token_gathers(src_hbm, dst, sem, count, chunks):
    pltpu.make_async_copy(src_hbm.at[pl.ds(0, count * chunks)], dst.at[pl.ds(0, count * chunks)], sem).wait()


def _experts_kernel(bnd_ref, tok_ref, x_hbm, gf_ref, wg_hbm, wu_hbm, wd_hbm, y_hbm,
                    xg, gsem, obuf, osem, wg32, wu32, wd32, wsem, wg16, wu16, wd16, *, rows, n_blocks):
    e = pl.program_id(0)
    ne = pl.num_programs(0)
    block_rows = obuf.shape[1]
    chunks = block_rows // rows
    pitch = _gather_pitch(chunks)
    n_used = bnd_ref[ne]

    def gather(b, slot, unrolled):
        base = b * rows
        _gather_tokens(lambda r: tok_ref[base + r], x_hbm, xg.at[slot], gsem.at[slot], rows, chunks, unrolled)

    def out_copy(b, slot):
        dst = y_hbm.at[pl.ds(pl.multiple_of(b * block_rows, block_rows), block_rows)]
        return pltpu.make_async_copy(obuf.at[slot], dst, osem.at[slot])

    def weight_copies(expert, slot):
        return [pltpu.make_async_copy(hbm.at[expert], buf.at[slot], wsem.at[slot])
                for hbm, buf in ((wg_hbm, wg32), (wu_hbm, wu32), (wd_hbm, wd32))]

    @pl.when(e == 0)
    def _():
        for cp in weight_copies(0, 0):
            cp.start(priority=1)
        gather(0, 0, unrolled=False)

    @pl.when(e + 1 < ne)
    def _():
        for cp in weight_copies(e + 1, (e + 1) % 2):
            cp.start(priority=1)

    for cp in weight_copies(e, e % 2):
        cp.wait()

    @pl.when(bnd_ref[e + 1] > bnd_ref[e])
    def _():
        wg16[...] = wg32[e % 2].astype(BF16)
        wu16[...] = wu32[e % 2].astype(BF16)
        wd16[...] = wd32[e % 2].astype(BF16)

    def block_body(b, carry):
        slot = b % 2

        @pl.when(b >= 2)
        def _():
            out_copy(b - 2, slot).wait()

        _wait_token_gathers(x_hbm, xg.at[slot], gsem.at[slot], rows, chunks)
        xb = _load_token_major(xg.at[slot], rows, chunks, pitch)
        gather(jnp.minimum(b + 1, n_blocks - 1), 1 - slot, unrolled=True)
        h = _rms(xb, gf_ref[...]).astype(BF16)
        hg = jnp.dot(h, wg16[...], preferred_element_type=F32)
        hu = jnp.dot(h, wu16[...], preferred_element_type=F32)
        hid = hg * (1.0 / (1.0 + jnp.exp(-hg))) * hu
        _store_token_major(obuf.at[slot], jnp.dot(hid.astype(BF16), wd16[...], preferred_element_type=F32))
        out_copy(b, slot).start()
        return carry

    lax.fori_loop(bnd_ref[e], bnd_ref[e + 1], block_body, 0)

    @pl.when(e == ne - 1)
    def _():
        _wait_token_gathers(x_hbm, xg.at[n_used % 2], gsem.at[n_used % 2], rows, chunks)

        @pl.when(n_used >= 2)
        def _():
            out_copy(n_used - 2, n_used % 2).wait()

        out_copy(n_used - 1, (n_used - 1) % 2).wait()
        obuf[0] = jnp.zeros(obuf.shape[1:], F32)

        def fill(b, carry):
            out_copy(b, 0).start()
            return carry

        def drain(b, carry):
            out_copy(b, 0).wait()
            return carry

        lax.fori_loop(n_used, n_blocks, fill, 0)
        lax.fori_loop(n_used, n_blocks, drain, 0)


def _experts(bounds, slot_tok, x2c, g_ffn, w_gate, w_up, w_down, block):
    n_blocks = slot_tok.shape[0] // block
    ne, d, de = w_gate.shape
    chunks = d // LANES
    grid_spec = pltpu.PrefetchScalarGridSpec(
        num_scalar_prefetch=2,
        grid=(ne,),
        in_specs=[pl.BlockSpec(memory_space=pl.ANY),
                  _resident(g_ffn.shape),
                  pl.BlockSpec(memory_space=pl.ANY),
                  pl.BlockSpec(memory_space=pl.ANY),
                  pl.BlockSpec(memory_space=pl.ANY)],
        out_specs=pl.BlockSpec(memory_space=pl.ANY),
        scratch_shapes=[pltpu.VMEM((2, block * _gather_pitch(chunks), LANES), F32), pltpu.SemaphoreType.DMA((2,)),
                        pltpu.VMEM((2, block * chunks, LANES), F32), pltpu.SemaphoreType.DMA((2,)),
                        pltpu.VMEM((2, d, de), F32), pltpu.VMEM((2, d, de), F32), pltpu.VMEM((2, de, d), F32),
                        pltpu.SemaphoreType.DMA((2,)),
                        pltpu.VMEM((d, de), BF16), pltpu.VMEM((d, de), BF16), pltpu.VMEM((de, d), BF16)],
    )
    return pl.pallas_call(
        functools.partial(_experts_kernel, rows=block, n_blocks=n_blocks),
        grid_spec=grid_spec,
        out_shape=jax.ShapeDtypeStruct((n_blocks * block * chunks, LANES), F32),
        compiler_params=pltpu.CompilerParams(dimension_semantics=("arbitrary",),
                                             vmem_limit_bytes=VMEM_LIMIT_BYTES),
        name="experts",
    )(bounds, slot_tok, x2c, g_ffn, w_gate, w_up, w_down)


def _combine_kernel(d_ref, dn_ref, x_ref, w_ref, y_hbm, g_ref, o_ref, ybuf, sems):
    i = pl.program_id(0)
    nt = pl.num_programs(0)
    tm, d = o_ref.shape
    chunks = d // LANES
    slot = i % 2

    def expert_rows(k):
        return _load_token_major(ybuf.at[slot, k], tm, chunks, _gather_pitch(chunks))

    def start(idx_ref, s):
        for k in range(TOP_K):
            _gather_tokens(lambda r, k=k: idx_ref[0, 0, k * tm + r], y_hbm, ybuf.at[s, k], sems.at[s], tm, chunks,
                           unrolled=False)

    def wait(s):
        for k in range(TOP_K):
            _wait_token_gathers(y_hbm, ybuf.at[s, k], sems.at[s], tm, chunks)

    @pl.when(i == 0)
    def _():
        start(d_ref, 0)

    start(dn_ref, 1 - slot)
    wait(slot)
    w = w_ref[...]
    y = _load_token_major(x_ref, tm, chunks, chunks) + (expert_rows(0) * w[:, 0:1] + expert_rows(1) * w[:, 1:2])
    o_ref[...] = _rms(y, g_ref[...])

    @pl.when(i == nt - 1)
    def _():
        wait(1 - slot)


def _combine(dest_tiles, x2c, w_tok, ysc, g_final):
    d = g_final.shape[1]
    chunks = d // LANES
    n = x2c.shape[0] // chunks
    tm = dest_tiles.shape[2] // TOP_K
    nt = n // tm
    smem_blk = lambda f: pl.BlockSpec((1, 1, TOP_K * tm), f, memory_space=pltpu.SMEM)
    return pl.pallas_call(
        _combine_kernel,
        grid=(nt,),
        in_specs=[smem_blk(lambda i: (i, 0, 0)), smem_blk(lambda i: (jnp.minimum(i + 1, nt - 1), 0, 0)),
                  pl.BlockSpec((tm * chunks, LANES), lambda i: (i, 0)),
                  pl.BlockSpec((tm, SUBLANES), lambda i: (i, 0)), pl.BlockSpec(memory_space=pl.ANY),
                  _resident(g_final.shape)],
        out_specs=pl.BlockSpec((tm, d), lambda i: (i, 0)),
        out_shape=jax.ShapeDtypeStruct((n, d), F32),
        scratch_shapes=[pltpu.VMEM((2, TOP_K, tm * _gather_pitch(chunks), LANES), F32),
                        pltpu.SemaphoreType.DMA((2,))],
        compiler_params=pltpu.CompilerParams(dimension_semantics=("arbitrary",),
                                             vmem_limit_bytes=VMEM_LIMIT_BYTES),
        name="combine",
    )(dest_tiles, dest_tiles, x2c, w_tok, ysc, g_final)


def _layer(x, mem, norm_mix_g, w_in, w_pool, pool_scale, w_dw, b_dw, conv_ln_g, conv_ln_b, w_conv_pw,
           out_norm_a_g, out_norm_b_g, w_out, norm_xattn_g, norm_mem_g, w_q_mem, w_kv_mem, w_o_mem, norm_ffn_g,
           w_router_group, b_router_group, w_router_expert, b_router_expert, w_exp_gate, w_exp_up, w_exp_down,
           out_gain):
    n, d = x.shape
    r2 = lambda v: v.reshape(1, -1)
    cw = w_conv_pw.shape[0]
    x1 = _mixer(x, r2(norm_mix_g), w_in.astype(BF16), w_pool.astype(BF16), r2(pool_scale),
                w_dw.reshape(CONV_KERNEL, cw), r2(b_dw),
                r2(conv_ln_g), r2(conv_ln_b), w_conv_pw.astype(BF16), r2(out_norm_a_g), r2(out_norm_b_g),
                w_out.astype(BF16))

    kv = _kv(mem, r2(norm_mem_g), w_kv_mem)
    w_router_t = jnp.zeros((LOGIT_ROWS, d), F32)
    w_router_t = w_router_t.at[:N_EXPERTS].set(w_router_expert.T).at[N_EXPERTS:N_EXPERTS + N_GROUPS].set(
        w_router_group.T).astype(BF16)
    qk, vo = _fold_memory(w_q_mem, w_o_mem, kv)
    x2, logits_t = _xattn(x1, r2(norm_xattn_g), qk, vo, r2(norm_ffn_g), w_router_t)

    block = MOE_BLOCK
    nb = pl.cdiv(n * TOP_K, block) + N_EXPERTS
    bias = jnp.zeros((ROUTE_ROWS, 1), F32)
    bias = bias.at[:N_EXPERTS, 0].set(b_router_expert).at[N_EXPERTS:N_EXPERTS + N_GROUPS, 0].set(b_router_group)
    dest, w_tok, blk, bst = _route(logits_t, bias, block, nb)
    dest = dest[:TOP_K]
    bounds = jnp.concatenate([bst[:, 0], blk[1, :1]])
    tok = jnp.broadcast_to(jnp.arange(n, dtype=I32), (TOP_K, n))
    slot_tok = (jnp.arange(nb * block, dtype=I32) % n).at[dest.reshape(-1)].set(
        tok.reshape(-1), unique_indices=True, mode="promise_in_bounds")

    ys = _experts(bounds, slot_tok, x2, r2(norm_ffn_g), w_exp_gate, w_exp_up, w_exp_down, block)

    tm = min(COMBINE_TILE, n)
    dest_tiles = dest.reshape(TOP_K, n // tm, tm).transpose(1, 0, 2).reshape(n // tm, 1, TOP_K * tm)
    return _combine(dest_tiles, x2, w_tok.T, ys, r2(out_gain))


def kernel(x, mem, norm_mix_g, w_in, w_pool, pool_scale, w_dw, b_dw, conv_ln_g, conv_ln_b, w_conv_pw, out_norm_a_g,
           out_norm_b_g, w_out, norm_xattn_g, norm_mem_g, w_q_mem, w_kv_mem, w_o_mem, norm_ffn_g, w_router_group,
           b_router_group, w_router_expert, b_router_expert, w_exp_gate, w_exp_up, w_exp_down, final_norm_g):
    assert x.shape[0] == 1 and mem.shape[0] == 1 and norm_mix_g.shape[0] == 1
    out = _layer(x[0], mem[0], norm_mix_g[0], w_in[0], w_pool[0], pool_scale[0], w_dw[0], b_dw[0], conv_ln_g[0],
                 conv_ln_b[0], w_conv_pw[0], out_norm_a_g[0], out_norm_b_g[0], w_out[0], norm_xattn_g[0],
                 norm_mem_g[0], w_q_mem[0], w_kv_mem[0], w_o_mem[0], norm_ffn_g[0], w_router_group[0],
                 b_router_group[0], w_router_expert[0], b_router_expert[0], w_exp_gate[0], w_exp_up[0],
                 w_exp_down[0], final_norm_g)
    return out[None]
```

```python
import functools

import jax
import jax.numpy as jnp
from jax import lax
from jax.experimental import pallas as pl
from jax.experimental.pallas import tpu as pltpu

F32 = jnp.float32
BF16 = jnp.bfloat16
I32 = jnp.int32

EPS = 1e-6
POOL_WINDOWS = (2, 4, 8, 16)
CONV_KERNEL = 31
XATTN_HEADS = 4
N_GROUPS = 4
EXPERTS_PER_GROUP = 8
N_EXPERTS = N_GROUPS * EXPERTS_PER_GROUP
TOP_K = 2

SUBLANES = 8
LANES = 128
VMEM_LIMIT_BYTES = 56 * 1024 * 1024

POOL_HALO = 16
TIME_PITCH = 2
CONV_HALO = 32
MIXER_TILE = 256
XATTN_TILE = 512
ROUTE_TILE = 512
MOE_BLOCK = 256
COMBINE_TILE = 256
LOGIT_ROWS = 128
ROUTE_ROWS = 40


def _rms(x, g):
    return x * lax.rsqrt(jnp.mean(x * x, axis=-1, keepdims=True) + EPS) * g


def _resident(shape):
    nd = len(shape)
    return pl.BlockSpec(shape, lambda *_: (0,) * nd, pipeline_mode=pl.Buffered(1))


def _store_token_major(ref, val):
    t, d = val.shape
    chunks = d // LANES
    for c in range(chunks):
        ref[pl.ds(c, t, stride=chunks), :] = val[:, c * LANES:(c + 1) * LANES]


def _load_token_major(ref, t, chunks, pitch):
    return jnp.concatenate([ref[pl.ds(c, t, stride=pitch), :] for c in range(chunks)], axis=-1)


def _gather_pitch(chunks):
    tiles = pl.cdiv(chunks, SUBLANES)
    return SUBLANES * (tiles + 1 - tiles % 2)


def _mixer_kernel(x_ref, gmix_ref, win_ref, wpool_ref, pscale_ref, wdw_ref, bdw_ref, lng_ref, lnb_ref,
                  wpw_ref, ga_ref, gb_ref, wout_ref, o_ref, hbuf, ubuf, ybuf, cbuf, cbf, ya_new, ycat):
    j = pl.program_id(0)
    tm = x_ref.shape[0]
    pw = ubuf.shape[0] * LANES
    cw = ybuf.shape[0] * LANES
    gdim = pw // len(POOL_WINDOWS)
    col = gdim

    @pl.when(j == 0)
    def _():
        ubuf[:, 0:TIME_PITCH * POOL_HALO, :] = jnp.zeros((pw // LANES, TIME_PITCH * POOL_HALO, LANES), F32)
        ybuf[:, 0:TIME_PITCH * CONV_HALO, :] = jnp.zeros((cw // LANES, TIME_PITCH * CONV_HALO, LANES), F32)

    def frames(buf, slab, r, count):
        return buf[slab, pl.ds(TIME_PITCH * r, count, stride=TIME_PITCH), :]

    def put_frames(buf, slab0, r, val):
        for s in range(val.shape[1] // LANES):
            buf[slab0 + s, pl.ds(TIME_PITCH * r, val.shape[0], stride=TIME_PITCH), :] = val[:, s * LANES:(s + 1) * LANES]

    def keep_halo(buf, halo):
        buf[:, 0:TIME_PITCH * halo, :] = buf[:, TIME_PITCH * tm:TIME_PITCH * (tm + halo), :]

    def norm_in():
        hbuf[...] = _rms(x_ref[...], gmix_ref[...]).astype(BF16)

    def proj_pool(c0):
        def run():
            u = jnp.dot(hbuf[...], win_ref[:, c0:c0 + col], preferred_element_type=F32)
            put_frames(ubuf, c0 // LANES, POOL_HALO, u)
        return run

    def proj_glu(c0):
        def run():
            a = jnp.dot(hbuf[...], win_ref[:, pw + c0:pw + c0 + col], preferred_element_type=F32)
            g = jnp.dot(hbuf[...], win_ref[:, pw + cw + c0:pw + cw + c0 + col], preferred_element_type=F32)
            put_frames(ybuf, c0 // LANES, CONV_HALO, a * (1.0 / (1.0 + jnp.exp(-g))))
        return run

    def pool_group(g, w):
        def run():
            pos = j * tm + lax.broadcasted_iota(I32, (tm, 1), 0)
            cols, toks = [], []
            for s in range(g * gdim // LANES, (g + 1) * gdim // LANES):
                tok = frames(ubuf, s, POOL_HALO, tm)
                wsum = tok
                for k in range(1, w):
                    wsum = wsum + frames(ubuf, s, POOL_HALO - k, tm)
                cols.append(wsum)
                toks.append(tok)
            cnt = jnp.minimum(pos + 1, w).astype(F32)
            d = jnp.concatenate(cols, axis=-1) / cnt - jnp.concatenate(toks, axis=-1)
            z = jnp.dot(d.astype(BF16), wpool_ref[g], preferred_element_type=F32)
            ya_new[:, g * gdim:(g + 1) * gdim] = z * pscale_ref[:, g * gdim:(g + 1) * gdim]
        return run

    def norm_pool():
        ya_new[...] = _rms(ya_new[...], ga_ref[...])

    first = CONV_HALO - (CONV_KERNEL - 1)
    row_chunk = 64

    def conv_chunk(r0, s):
        def run():
            acc = None
            for k in range(CONV_KERNEL):
                term = frames(ybuf, s, first + r0 + k, row_chunk) * wdw_ref[k:k + 1, s * LANES:(s + 1) * LANES]
                acc = term if acc is None else acc + term
            cbuf[r0:r0 + row_chunk, s * LANES:(s + 1) * LANES] = acc
        return run

    def norm_conv():
        c = cbuf[...] + bdw_ref[...]
        mu = jnp.mean(c, axis=-1, keepdims=True)
        var = jnp.mean(jnp.square(c - mu), axis=-1, keepdims=True)
        c = (c - mu) * lax.rsqrt(var + EPS) * lng_ref[...] + lnb_ref[...]
        cbf[...] = (c * (1.0 / (1.0 + jnp.exp(-c)))).astype(BF16)

    def pointwise():
        yb = jnp.dot(cbf[...], wpw_ref[...], preferred_element_type=F32)
        ycat[:, 0:pw] = ya_new[...].astype(BF16)
        ycat[:, pw:pw + cw] = _rms(yb, gb_ref[...]).astype(BF16)

    def out_cols(c0):
        def run():
            o_ref[:, c0:c0 + col] = x_ref[:, c0:c0 + col] + jnp.dot(
                ycat[...], wout_ref[:, c0:c0 + col], preferred_element_type=F32)
        return run

    norm_in()
    for c0 in range(0, cw, col):
        if c0 < pw:
            proj_pool(c0)()
        proj_glu(c0)()
        if c0 < pw:
            pool_group(c0 // gdim, POOL_WINDOWS[c0 // gdim])()
        for s in range(c0 // LANES, (c0 + col) // LANES):
            for r0 in range(0, tm, row_chunk):
                conv_chunk(r0, s)()
    norm_pool()
    norm_conv()
    pointwise()
    for c0 in range(0, x_ref.shape[1], col):
        out_cols(c0)()
    keep_halo(ubuf, POOL_HALO)
    keep_halo(ybuf, CONV_HALO)


def _mixer(x, g_mix, w_in, w_pool, pool_scale, w_dw, b_dw, ln_g, ln_b, w_pw, g_a, g_b, w_out):
    n, d = x.shape
    pw = w_pool.shape[0] * w_pool.shape[1]
    cw = w_pw.shape[0]
    tm = min(MIXER_TILE, n)
    assert cw % (pw // len(POOL_WINDOWS)) == 0 and pw <= cw
    row = pl.BlockSpec((tm, d), lambda j: (j, 0))
    return pl.pallas_call(
        _mixer_kernel,
        grid=(n // tm,),
        in_specs=[row, _resident(g_mix.shape), _resident(w_in.shape), _resident(w_pool.shape),
                  _resident(pool_scale.shape), _resident(w_dw.shape), _resident(b_dw.shape),
                  _resident(ln_g.shape), _resident(ln_b.shape), _resident(w_pw.shape),
                  _resident(g_a.shape), _resident(g_b.shape), _resident(w_out.shape)],
        out_specs=row,
        out_shape=jax.ShapeDtypeStruct((n, d), F32),
        scratch_shapes=[pltpu.VMEM((tm, d), BF16),
                        pltpu.VMEM((pw // LANES, TIME_PITCH * (tm + POOL_HALO), LANES), F32),
                        pltpu.VMEM((cw // LANES, TIME_PITCH * (tm + CONV_HALO), LANES), F32),
                        pltpu.VMEM((tm, cw), F32),
                        pltpu.VMEM((tm, cw), BF16),
                        pltpu.VMEM((tm, pw), F32),
                        pltpu.VMEM((tm, pw + cw), BF16)],
        compiler_params=pltpu.CompilerParams(dimension_semantics=("arbitrary",),
                                             vmem_limit_bytes=VMEM_LIMIT_BYTES),
        name="mixer",
    )(x, g_mix, w_in, w_pool, pool_scale, w_dw, b_dw, ln_g, ln_b, w_pw, g_a, g_b, w_out)


def _kv_kernel(mem_ref, g_ref, w_ref, o_ref):
    m = _rms(mem_ref[...], g_ref[...])
    o_ref[...] = jnp.dot(m.astype(BF16), w_ref[...].astype(BF16), preferred_element_type=F32).astype(BF16)


def _kv(mem, g_mem, w_kv):
    m, d = mem.shape
    n_out = w_kv.shape[1]
    bn = 512
    return pl.pallas_call(
        _kv_kernel,
        grid=(n_out // bn,),
        in_specs=[_resident((m, d)), _resident(g_mem.shape), pl.BlockSpec((d, bn), lambda j: (0, j))],
        out_specs=pl.BlockSpec((m, bn), lambda j: (0, j)),
        out_shape=jax.ShapeDtypeStruct((m, n_out), BF16),
        compiler_params=pltpu.CompilerParams(dimension_semantics=("arbitrary",),
                                             vmem_limit_bytes=VMEM_LIMIT_BYTES),
        name="kv",
    )(mem, g_mem, w_kv)


def _fold_kernel(wq_ref, wo_ref, kv_ref, a_ref, b_ref):
    hd = wq_ref.shape[1]
    d = wq_ref.shape[0]
    hh = pl.program_id(0)
    col = pl.multiple_of(hh * hd, hd)
    kh = kv_ref[:, pl.ds(col, hd)]
    vh = kv_ref[:, pl.ds(pl.multiple_of(d + hh * hd, hd), hd)]
    a_ref[...] = lax.dot_general(wq_ref[...].astype(BF16), kh, (((1,), (1,)), ((), ())),
                                 preferred_element_type=F32).astype(BF16)
    b_ref[...] = jnp.dot(vh, wo_ref[...].astype(BF16), preferred_element_type=F32).astype(BF16)


def _fold_memory(w_q, w_o, kv):
    d = w_q.shape[0]
    m = kv.shape[0]
    hd = d // XATTN_HEADS
    return pl.pallas_call(
        _fold_kernel,
        grid=(XATTN_HEADS,),
        in_specs=[pl.BlockSpec((d, hd), lambda h: (0, h)), pl.BlockSpec((hd, d), lambda h: (h, 0)),
                  _resident(kv.shape)],
        out_specs=[pl.BlockSpec((d, m), lambda h: (0, h)), pl.BlockSpec((m, d), lambda h: (h, 0))],
        out_shape=[jax.ShapeDtypeStruct((d, XATTN_HEADS * m), BF16), jax.ShapeDtypeStruct((XATTN_HEADS * m, d), BF16)],
        compiler_params=pltpu.CompilerParams(dimension_semantics=("arbitrary",),
                                             vmem_limit_bytes=VMEM_LIMIT_BYTES),
        name="fold",
    )(w_q, w_o, kv)


def _xattn_kernel(x_ref, gx_ref, a_ref, b_ref, gf_ref, wr_ref, o_ref, lg_ref):
    d = x_ref.shape[1]
    hd = d // XATTN_HEADS
    m = a_ref.shape[1] // XATTN_HEADS
    x = x_ref[...]
    h = _rms(x, gx_ref[...])
    s_all = jnp.dot(h.astype(BF16), a_ref[...], preferred_element_type=F32)
    ps = []
    for hh in range(XATTN_HEADS):
        s = s_all[:, hh * m:(hh + 1) * m] * (hd ** -0.5)
        e = jnp.exp(s - jnp.max(s, axis=-1, keepdims=True))
        ps.append((e / jnp.sum(e, axis=-1, keepdims=True)).astype(BF16))
    x2 = x + jnp.dot(jnp.concatenate(ps, axis=-1), b_ref[...], preferred_element_type=F32)
    _store_token_major(o_ref, x2)
    h3 = _rms(x2, gf_ref[...])
    lg_ref[...] = lax.dot_general(wr_ref[...], h3.astype(BF16), (((1,), (1,)), ((), ())),
                                  preferred_element_type=F32)


def _xattn(x, g_x, qk, vo, g_ffn, w_router_t):
    n, d = x.shape
    tm = min(XATTN_TILE, n)
    chunks = d // LANES
    row = pl.BlockSpec((tm, d), lambda i: (i, 0))
    return pl.pallas_call(
        _xattn_kernel,
        grid=(n // tm,),
        in_specs=[row, _resident(g_x.shape), _resident(qk.shape), _resident(vo.shape),
                  _resident(g_ffn.shape), _resident(w_router_t.shape)],
        out_specs=[pl.BlockSpec((tm * chunks, LANES), lambda i: (i, 0)),
                   pl.BlockSpec((LOGIT_ROWS, tm), lambda i: (0, i))],
        out_shape=[jax.ShapeDtypeStruct((n * chunks, LANES), F32), jax.ShapeDtypeStruct((LOGIT_ROWS, n), F32)],
        compiler_params=pltpu.CompilerParams(dimension_semantics=("arbitrary",),
                                             vmem_limit_bytes=VMEM_LIMIT_BYTES),
        name="xattn",
    )(x, g_x, qk, vo, g_ffn, w_router_t)


def _first_argmax(v, rows):
    m = jnp.max(v, axis=0, keepdims=True)
    idx = jnp.min(jnp.where(v == m, rows.astype(F32), float(v.shape[0])), axis=0, keepdims=True)
    return m, idx.astype(I32)


def _route_kernel(lg_ref, bias_ref, tri_ref, etri_ref, dest_ref, wtok_ref, blk_ref, bst_ref, eg_buf, rank_buf,
                  *, block):
    n = lg_ref.shape[1]
    t = tri_ref.shape[0]
    n_tiles = n // t
    epg = EXPERTS_PER_GROUP
    rows_g = lax.broadcasted_iota(I32, (N_GROUPS, t), 0)
    rows_e = lax.broadcasted_iota(I32, (epg, t), 0)
    rows_all = lax.broadcasted_iota(I32, (N_EXPERTS, t), 0)

    def tile_body(j, carry):
        sl = pl.ds(pl.multiple_of(j * t, t), t)
        le = lg_ref[0:N_EXPERTS, sl] + bias_ref[0:N_EXPERTS, :]
        lgp = lg_ref[N_EXPERTS:N_EXPERTS + N_GROUPS, sl] + bias_ref[N_EXPERTS:N_EXPERTS + N_GROUPS, :]
        eg_ = jnp.exp(lgp - jnp.max(lgp, axis=0, keepdims=True))
        pg = eg_ / jnp.sum(eg_, axis=0, keepdims=True)
        pg_top, g_idx = _first_argmax(pg, rows_g)
        le_sel = jnp.zeros((epg, t), F32)
        for g in range(N_GROUPS):
            le_sel = jnp.where(g_idx == g, le[g * epg:(g + 1) * epg, :], le_sel)
        ee = jnp.exp(le_sel - jnp.max(le_sel, axis=0, keepdims=True))
        pe = ee / jnp.sum(ee, axis=0, keepdims=True)
        p1, i1 = _first_argmax(pe, rows_e)
        p2, i2 = _first_argmax(jnp.where(rows_e == i1, -1.0, pe), rows_e)
        psum = p1 + p2
        w1 = pg_top * p1 / psum
        w2 = pg_top * p2 / psum
        e1 = g_idx * epg + i1
        e2 = g_idx * epg + i2
        oh1 = rows_all == e1
        oh2 = rows_all == e2
        onehot = jnp.where(oh1 | oh2, 1.0, 0.0)
        prefix = carry + jnp.dot(onehot.astype(BF16), tri_ref[...], preferred_element_type=F32)
        r1 = jnp.sum(jnp.where(oh1, prefix, 0.0), axis=0, keepdims=True)
        r2 = jnp.sum(jnp.where(oh2, prefix, 0.0), axis=0, keepdims=True)
        eg_buf[0:1, sl] = e1
        eg_buf[1:2, sl] = e2
        rank_buf[0:1, sl] = r1
        rank_buf[1:2, sl] = r2
        wtok_ref[0:1, sl] = w1
        wtok_ref[1:2, sl] = w2
        return carry + jnp.sum(onehot, axis=1, keepdims=True)

    counts = lax.fori_loop(0, n_tiles, tile_body, jnp.zeros((N_EXPERTS, 1), F32))
    wtok_ref[2:SUBLANES, :] = jnp.zeros((SUBLANES - 2, n), F32)

    nblk = jnp.floor((counts + (block - 1)) * (1.0 / block))
    nblk_b = jnp.broadcast_to(nblk, (N_EXPERTS, LANES)).astype(BF16)
    bstart = jnp.dot(etri_ref[...], nblk_b, preferred_element_type=F32)[:, 0:1]
    bend = bstart + nblk
    pstart = bstart * block

    def dest_body(j, _):
        sl = pl.ds(pl.multiple_of(j * t, t), t)
        for k in range(TOP_K):
            oh = rows_all == eg_buf[k:k + 1, sl]
            base = jnp.sum(jnp.where(oh, pstart, 0.0), axis=0, keepdims=True)
            dest_ref[k:k + 1, sl] = (base + rank_buf[k:k + 1, sl]).astype(I32)
        return 0

    lax.fori_loop(0, n_tiles, dest_body, 0)
    dest_ref[2:SUBLANES, :] = jnp.zeros((SUBLANES - 2, n), I32)

    nbp = blk_ref.shape[1]
    bidx = lax.broadcasted_iota(I32, (N_EXPERTS, nbp), 1).astype(F32)
    be = jnp.sum(jnp.where(bend <= bidx, 1.0, 0.0), axis=0, keepdims=True)
    n_used = jnp.max(bend, axis=0, keepdims=True)
    blk_ref[0:1, :] = be.astype(I32)
    blk_ref[1:2, :] = jnp.broadcast_to(n_used, (1, nbp)).astype(I32)
    blk_ref[2:SUBLANES, :] = jnp.zeros((SUBLANES - 2, nbp), I32)
    bst_ref[...] = jnp.broadcast_to(bstart, bst_ref.shape).astype(I32)


def _route(logits_t, bias, block, nb):
    n = logits_t.shape[1]
    t = min(ROUTE_TILE, n)
    nbp = pl.cdiv(nb, LANES) * LANES
    assert nbp <= 256, "block counts must stay exactly representable in bf16"
    tri = (lax.broadcasted_iota(I32, (t, t), 0) < lax.broadcasted_iota(I32, (t, t), 1)).astype(BF16)
    etri = (lax.broadcasted_iota(I32, (N_EXPERTS, N_EXPERTS), 1)
            < lax.broadcasted_iota(I32, (N_EXPERTS, N_EXPERTS), 0)).astype(BF16)
    return pl.pallas_call(
        functools.partial(_route_kernel, block=block),
        grid=(1,),
        in_specs=[pl.BlockSpec((ROUTE_ROWS, n), lambda i: (0, 0)), _resident(bias.shape), _resident(tri.shape),
                  _resident(etri.shape)],
        out_specs=[pl.BlockSpec((SUBLANES, n), lambda i: (0, 0)), pl.BlockSpec((SUBLANES, n), lambda i: (0, 0)),
                   pl.BlockSpec((SUBLANES, nbp), lambda i: (0, 0)),
                   pl.BlockSpec((N_EXPERTS, LANES), lambda i: (0, 0))],
        out_shape=[jax.ShapeDtypeStruct((SUBLANES, n), I32), jax.ShapeDtypeStruct((SUBLANES, n), F32),
                   jax.ShapeDtypeStruct((SUBLANES, nbp), I32), jax.ShapeDtypeStruct((N_EXPERTS, LANES), I32)],
        scratch_shapes=[pltpu.VMEM((SUBLANES, n), I32), pltpu.VMEM((SUBLANES, n), F32)],
        compiler_params=pltpu.CompilerParams(dimension_semantics=("arbitrary",),
                                             vmem_limit_bytes=VMEM_LIMIT_BYTES),
        name="route",
    )(logits_t, bias, tri, etri)


def _start_token_gather(tok, src_hbm, dst, sem, row0, chunks):
    pltpu.make_async_copy(src_hbm.at[pl.ds(pl.multiple_of(tok * chunks, chunks), chunks)],
                          dst.at[pl.ds(row0, chunks)], sem).start()


def _gather_tokens(token_of, src_hbm, dst, sem, count, chunks, unrolled):
    pitch = _gather_pitch(chunks)
    if unrolled:
        for r in range(count):
            _start_token_gather(token_of(r), src_hbm, dst, sem, r * pitch, chunks)
    else:
        def body(r, _):
            _start_token_gather(token_of(r), src_hbm, dst, sem, pl.multiple_of(r * pitch, SUBLANES), chunks)
            return 0
        lax.fori_loop(0, count, body, 0, unroll=8)


def _wait_token_gathers(src_hbm, dst, sem, count, chunks):
    pltpu.make_async_copy(src_hbm.at[pl.ds(0, count * chunks)], dst.at[pl.ds(0, count * chunks)], sem).wait()


def _experts_kernel(bnd_ref, tok_ref, x_hbm, gf_ref, wg_hbm, wu_hbm, wd_hbm, y_hbm,
                    xg, gsem, obuf, osem, wg32, wu32, wd32, wsem, wg16, wu16, wd16, *, rows, n_blocks):
    e = pl.program_id(0)
    ne = pl.num_programs(0)
    block_rows = obuf.shape[1]
    chunks = block_rows // rows
    pitch = _gather_pitch(chunks)
    n_used = bnd_ref[ne]

    def gather(b, slot, unrolled):
        base = b * rows
        _gather_tokens(lambda r: tok_ref[base + r], x_hbm, xg.at[slot], gsem.at[slot], rows, chunks, unrolled)

    def out_copy(b, slot):
        dst = y_hbm.at[pl.ds(pl.multiple_of(b * block_rows, block_rows), block_rows)]
        return pltpu.make_async_copy(obuf.at[slot], dst, osem.at[slot])

    def weight_copies(expert, slot):
        return [pltpu.make_async_copy(hbm.at[expert], buf.at[slot], wsem.at[slot])
                for hbm, buf in ((wg_hbm, wg32), (wu_hbm, wu32), (wd_hbm, wd32))]

    @pl.when(e == 0)
    def _():
        for cp in weight_copies(0, 0):
            cp.start(priority=1)
        gather(0, 0, unrolled=False)

    @pl.when(e + 1 < ne)
    def _():
        for cp in weight_copies(e + 1, (e + 1) % 2):
            cp.start(priority=1)

    for cp in weight_copies(e, e % 2):
        cp.wait()

    @pl.when(bnd_ref[e + 1] > bnd_ref[e])
    def _():
        wg16[...] = wg32[e % 2].astype(BF16)
        wu16[...] = wu32[e % 2].astype(BF16)
        wd16[...] = wd32[e % 2].astype(BF16)

    def block_body(b, carry):
        slot = b % 2

        @pl.when(b >= 2)
        def _():
            out_copy(b - 2, slot).wait()

        _wait_token_gathers(x_hbm, xg.at[slot], gsem.at[slot], rows, chunks)
        xb = _load_token_major(xg.at[slot], rows, chunks, pitch)
        gather(jnp.minimum(b + 1, n_blocks - 1), 1 - slot, unrolled=True)
        h = _rms(xb, gf_ref[...]).astype(BF16)
        hg = jnp.dot(h, wg16[...], preferred_element_type=F32)
        hu = jnp.dot(h, wu16[...], preferred_element_type=F32)
        hid = hg * (1.0 / (1.0 + jnp.exp(-hg))) * hu
        _store_token_major(obuf.at[slot], jnp.dot(hid.astype(BF16), wd16[...], preferred_element_type=F32))
        out_copy(b, slot).start()
        return carry

    lax.fori_loop(bnd_ref[e], bnd_ref[e + 1], block_body, 0)

    @pl.when(e == ne - 1)
    def _():
        _wait_token_gathers(x_hbm, xg.at[n_used % 2], gsem.at[n_used % 2], rows, chunks)

        @pl.when(n_used >= 2)
        def _():
            out_copy(n_used - 2, n_used % 2).wait()

        out_copy(n_used - 1, (n_used - 1) % 2).wait()
        obuf[0] = jnp.zeros(obuf.shape[1:], F32)

        def fill(b, carry):
            out_copy(b, 0).start()
            return carry

        def drain(b, carry):
            out_copy(b, 0).wait()
            return carry

        lax.fori_loop(n_used, n_blocks, fill, 0)
        lax.fori_loop(n_used, n_blocks, drain, 0)


def _experts(bounds, slot_tok, x2c, g_ffn, w_gate, w_up, w_down, block):
    n_blocks = slot_tok.shape[0] // block
    ne, d, de = w_gate.shape
    chunks = d // LANES
    grid_spec = pltpu.PrefetchScalarGridSpec(
        num_scalar_prefetch=2,
        grid=(ne,),
        in_specs=[pl.BlockSpec(memory_space=pl.ANY),
                  _resident(g_ffn.shape),
                  pl.BlockSpec(memory_space=pl.ANY),
                  pl.BlockSpec(memory_space=pl.ANY),
                  pl.BlockSpec(memory_space=pl.ANY)],
        out_specs=pl.BlockSpec(memory_space=pl.ANY),
        scratch_shapes=[pltpu.VMEM((2, block * _gather_pitch(chunks), LANES), F32), pltpu.SemaphoreType.DMA((2,)),
                        pltpu.VMEM((2, block * chunks, LANES), F32), pltpu.SemaphoreType.DMA((2,)),
                        pltpu.VMEM((2, d, de), F32), pltpu.VMEM((2, d, de), F32), pltpu.VMEM((2, de, d), F32),
                        pltpu.SemaphoreType.DMA((2,)),
                        pltpu.VMEM((d, de), BF16), pltpu.VMEM((d, de), BF16), pltpu.VMEM((de, d), BF16)],
    )
    return pl.pallas_call(
        functools.partial(_experts_kernel, rows=block, n_blocks=n_blocks),
        grid_spec=grid_spec,
        out_shape=jax.ShapeDtypeStruct((n_blocks * block * chunks, LANES), F32),
        compiler_params=pltpu.CompilerParams(dimension_semantics=("arbitrary",),
                                             vmem_limit_bytes=VMEM_LIMIT_BYTES),
        name="experts",
    )(bounds, slot_tok, x2c, g_ffn, w_gate, w_up, w_down)


def _combine_kernel(d_ref, dn_ref, x_ref, w_ref, y_hbm, g_ref, o_ref, ybuf, sems):
    i = pl.program_id(0)
    nt = pl.num_programs(0)
    tm, d = o_ref.shape
    chunks = d // LANES
    slot = i % 2

    def expert_rows(k):
        return _load_token_major(ybuf.at[slot, k], tm, chunks, _gather_pitch(chunks))

    def start(idx_ref, s):
        for k in range(TOP_K):
            _gather_tokens(lambda r, k=k: idx_ref[0, 0, k * tm + r], y_hbm, ybuf.at[s, k], sems.at[s], tm, chunks,
                           unrolled=False)

    def wait(s):
        for k in range(TOP_K):
            _wait_token_gathers(y_hbm, ybuf.at[s, k], sems.at[s], tm, chunks)

    @pl.when(i == 0)
    def _():
        start(d_ref, 0)

    start(dn_ref, 1 - slot)
    wait(slot)
    w = w_ref[...]
    y = _load_token_major(x_ref, tm, chunks, chunks) + (expert_rows(0) * w[:, 0:1] + expert_rows(1) * w[:, 1:2])
    o_ref[...] = _rms(y, g_ref[...])

    @pl.when(i == nt - 1)
    def _():
        wait(1 - slot)


def _combine(dest_tiles, x2c, w_tok, ysc, g_final):
    d = g_final.shape[1]
    chunks = d // LANES
    n = x2c.shape[0] // chunks
    tm = dest_tiles.shape[2] // TOP_K
    nt = n // tm
    smem_blk = lambda f: pl.BlockSpec((1, 1, TOP_K * tm), f, memory_space=pltpu.SMEM)
    return pl.pallas_call(
        _combine_kernel,
        grid=(nt,),
        in_specs=[smem_blk(lambda i: (i, 0, 0)), smem_blk(lambda i: (jnp.minimum(i + 1, nt - 1), 0, 0)),
                  pl.BlockSpec((tm * chunks, LANES), lambda i: (i, 0)),
                  pl.BlockSpec((tm, SUBLANES), lambda i: (i, 0)), pl.BlockSpec(memory_space=pl.ANY),
                  _resident(g_final.shape)],
        out_specs=pl.BlockSpec((tm, d), lambda i: (i, 0)),
        out_shape=jax.ShapeDtypeStruct((n, d), F32),
        scratch_shapes=[pltpu.VMEM((2, TOP_K, tm * _gather_pitch(chunks), LANES), F32),
                        pltpu.SemaphoreType.DMA((2,))],
        compiler_params=pltpu.CompilerParams(dimension_semantics=("arbitrary",),
                                             vmem_limit_bytes=VMEM_LIMIT_BYTES),
        name="combine",
    )(dest_tiles, dest_tiles, x2c, w_tok, ysc, g_final)


def _layer(x, mem, norm_mix_g, w_in, w_pool, pool_scale, w_dw, b_dw, conv_ln_g, conv_ln_b, w_conv_pw,
           out_norm_a_g, out_norm_b_g, w_out, norm_xattn_g, norm_mem_g, w_q_mem, w_kv_mem, w_o_mem, norm_ffn_g,
           w_router_group, b_router_group, w_router_expert, b_router_expert, w_exp_gate, w_exp_up, w_exp_down,
           out_gain):
    n, d = x.shape
    r2 = lambda v: v.reshape(1, -1)
    cw = w_conv_pw.shape[0]
    x1 = _mixer(x, r2(norm_mix_g), w_in.astype(BF16), w_pool.astype(BF16), r2(pool_scale),
                w_dw.reshape(CONV_KERNEL, cw), r2(b_dw),
                r2(conv_ln_g), r2(conv_ln_b), w_conv_pw.astype(BF16), r2(out_norm_a_g), r2(out_norm_b_g),
                w_out.astype(BF16))

    kv = _kv(mem, r2(norm_mem_g), w_kv_mem)
    w_router_t = jnp.zeros((LOGIT_ROWS, d), F32)
    w_router_t = w_router_t.at[:N_EXPERTS].set(w_router_expert.T).at[N_EXPERTS:N_EXPERTS + N_GROUPS].set(
        w_router_group.T).astype(BF16)
    qk, vo = _fold_memory(w_q_mem, w_o_mem, kv)
    x2, logits_t = _xattn(x1, r2(norm_xattn_g), qk, vo, r2(norm_ffn_g), w_router_t)

    block = MOE_BLOCK
    nb = pl.cdiv(n * TOP_K, block) + N_EXPERTS
    bias = jnp.zeros((ROUTE_ROWS, 1), F32)
    bias = bias.at[:N_EXPERTS, 0].set(b_router_expert).at[N_EXPERTS:N_EXPERTS + N_GROUPS, 0].set(b_router_group)
    dest, w_tok, blk, bst = _route(logits_t, bias, block, nb)
    dest = dest[:TOP_K]
    bounds = jnp.concatenate([bst[:, 0], blk[1, :1]])
    tok = jnp.broadcast_to(jnp.arange(n, dtype=I32), (TOP_K, n))
    slot_tok = (jnp.arange(nb * block, dtype=I32) % n).at[dest.reshape(-1)].set(
        tok.reshape(-1), unique_indices=True, mode="promise_in_bounds")

    ys = _experts(bounds, slot_tok, x2, r2(norm_ffn_g), w_exp_gate, w_exp_up, w_exp_down, block)

    tm = min(COMBINE_TILE, n)
    dest_tiles = dest.reshape(TOP_K, n // tm, tm).transpose(1, 0, 2).reshape(n // tm, 1, TOP_K * tm)
    return _combine(dest_tiles, x2, w_tok.T, ys, r2(out_gain))


def kernel(x, mem, norm_mix_g, w_in, w_pool, pool_scale, w_dw, b_dw, conv_ln_g, conv_ln_b, w_conv_pw, out_norm_a_g,
           out_norm_b_g, w_out, norm_xattn_g, norm_mem_g, w_q_mem, w_kv_mem, w_o_mem, norm_ffn_g, w_router_group,
           b_router_group, w_router_expert, b_router_expert, w_exp_gate, w_exp_up, w_exp_down, final_norm_g):
    assert x.shape[0] == 1 and mem.shape[0] == 1 and norm_mix_g.shape[0] == 1
    out = _layer(x[0], mem[0], norm_mix_g[0], w_in[0], w_pool[0], pool_scale[0], w_dw[0], b_dw[0], conv_ln_g[0],
                 conv_ln_b[0], w_conv_pw[0], out_norm_a_g[0], out_norm_b_g[0], w_out[0], norm_xattn_g[0],
                 norm_mem_g[0], w_q_mem[0], w_kv_mem[0], w_o_mem[0], norm_ffn_g[0], w_router_group[0],
                 b_router_group[0], w_router_expert[0], b_router_expert[0], w_exp_gate[0], w_exp_up[0],
                 w_exp_down[0], final_norm_g)
    return out[None]
```

```python
import functools

import jax
import jax.numpy as jnp
from jax import lax
from jax.experimental import pallas as pl
from jax.experimental.pallas import tpu as pltpu

F32 = jnp.float32
BF16 = jnp.bfloat16
I32 = jnp.int32

EPS = 1e-6
POOL_WINDOWS = (2, 4, 8, 16)
CONV_KERNEL = 31
XATTN_HEADS = 4
N_GROUPS = 4
EXPERTS_PER_GROUP = 8
N_EXPERTS = N_GROUPS * EXPERTS_PER_GROUP
TOP_K = 2

SUBLANES = 8
LANES = 128
VMEM_LIMIT_BYTES = 56 * 1024 * 1024

POOL_HALO = 16
TIME_PITCH = 2
CONV_HALO = 32
MIXER_TILE = 256
XATTN_TILE = 512
ROUTE_TILE = 512
MOE_BLOCK = 256
COMBINE_TILE = 256
LOGIT_ROWS = 128
ROUTE_ROWS = 40


def _rms(x, g):
    return x * lax.rsqrt(jnp.mean(x * x, axis=-1, keepdims=True) + EPS) * g


def _resident(shape):
    nd = len(shape)
    return pl.BlockSpec(shape, lambda *_: (0,) * nd, pipeline_mode=pl.Buffered(1))


def _store_token_major(ref, val):
    t, d = val.shape
    chunks = d // LANES
    for c in range(chunks):
        ref[pl.ds(c, t, stride=chunks), :] = val[:, c * LANES:(c + 1) * LANES]


def _load_token_major(ref, t, chunks, pitch):
    return jnp.concatenate([ref[pl.ds(c, t, stride=pitch), :] for c in range(chunks)], axis=-1)


def _gather_pitch(chunks):
    tiles = pl.cdiv(chunks, SUBLANES)
    return SUBLANES * (tiles + 1 - tiles % 2)


def _mixer_kernel(x_ref, gmix_ref, win_ref, wpool_ref, pscale_ref, wdw_ref, bdw_ref, lng_ref, lnb_ref,
                  wpw_ref, ga_ref, gb_ref, wout_ref, o_ref, hbuf, ubuf, ybuf, cbuf, cbf, ya_new, ycat):
    j = pl.program_id(0)
    tm = x_ref.shape[0]
    pw = ubuf.shape[0] * LANES
    cw = ybuf.shape[0] * LANES
    gdim = pw // len(POOL_WINDOWS)
    col = gdim

    @pl.when(j == 0)
    def _():
        ubuf[:, 0:TIME_PITCH * POOL_HALO, :] = jnp.zeros((pw // LANES, TIME_PITCH * POOL_HALO, LANES), F32)
        ybuf[:, 0:TIME_PITCH * CONV_HALO, :] = jnp.zeros((cw // LANES, TIME_PITCH * CONV_HALO, LANES), F32)

    def frames(buf, slab, r, count):
        return buf[slab, pl.ds(TIME_PITCH * r, count, stride=TIME_PITCH), :]

    def put_frames(buf, slab0, r, val):
        for s in range(val.shape[1] // LANES):
            buf[slab0 + s, pl.ds(TIME_PITCH * r, val.shape[0], stride=TIME_PITCH), :] = val[:, s * LANES:(s + 1) * LANES]

    def keep_halo(buf, halo):
        buf[:, 0:TIME_PITCH * halo, :] = buf[:, TIME_PITCH * tm:TIME_PITCH * (tm + halo), :]

    def norm_in():
        hbuf[...] = _rms(x_ref[...], gmix_ref[...]).astype(BF16)

    def proj_pool(c0):
        def run():
            u = jnp.dot(hbuf[...], win_ref[:, c0:c0 + col], preferred_element_type=F32)
            put_frames(ubuf, c0 // LANES, POOL_HALO, u)
        return run

    def proj_glu(c0):
        def run():
            a = jnp.dot(hbuf[...], win_ref[:, pw + c0:pw + c0 + col], preferred_element_type=F32)
            g = jnp.dot(hbuf[...], win_ref[:, pw + cw + c0:pw + cw + c0 + col], preferred_element_type=F32)
            put_frames(ybuf, c0 // LANES, CONV_HALO, a * (1.0 / (1.0 + jnp.exp(-g))))
        return run

    def pool_group(g, w):
        def run():
            pos = j * tm + lax.broadcasted_iota(I32, (tm, 1), 0)
            cols, toks = [], []
            for s in range(g * gdim // LANES, (g + 1) * gdim // LANES):
                tok = frames(ubuf, s, POOL_HALO, tm)
                wsum = tok
                for k in range(1, w):
                    wsum = wsum + frames(ubuf, s, POOL_HALO - k, tm)
                cols.append(wsum)
                toks.append(tok)
            cnt = jnp.minimum(pos + 1, w).astype(F32)
            d = jnp.concatenate(cols, axis=-1) / cnt - jnp.concatenate(toks, axis=-1)
            z = jnp.dot(d.astype(BF16), wpool_ref[g], preferred_element_type=F32)
            ya_new[:, g * gdim:(g + 1) * gdim] = z * pscale_ref[:, g * gdim:(g + 1) * gdim]
        return run

    def norm_pool():
        ya_new[...] = _rms(ya_new[...], ga_ref[...])

    first = CONV_HALO - (CONV_KERNEL - 1)
    row_chunk = 64

    def conv_chunk(r0, s):
        def run():
            acc = None
            for k in range(CONV_KERNEL):
                term = frames(ybuf, s, first + r0 + k, row_chunk) * wdw_ref[k:k + 1, s * LANES:(s + 1) * LANES]
                acc = term if acc is None else acc + term
            cbuf[r0:r0 + row_chunk, s * LANES:(s + 1) * LANES] = acc
        return run

    def norm_conv():
        c = cbuf[...] + bdw_ref[...]
        mu = jnp.mean(c, axis=-1, keepdims=True)
        var = jnp.mean(jnp.square(c - mu), axis=-1, keepdims=True)
        c = (c - mu) * lax.rsqrt(var + EPS) * lng_ref[...] + lnb_ref[...]
        cbf[...] = (c * (1.0 / (1.0 + jnp.exp(-c)))).astype(BF16)

    def pointwise():
        yb = jnp.dot(cbf[...], wpw_ref[...], preferred_element_type=F32)
        ycat[:, 0:pw] = ya_new[...].astype(BF16)
        ycat[:, pw:pw + cw] = _rms(yb, gb_ref[...]).astype(BF16)

    def out_cols(c0):
        def run():
            o_ref[:, c0:c0 + col] = x_ref[:, c0:c0 + col] + jnp.dot(
                ycat[...], wout_ref[:, c0:c0 + col], preferred_element_type=F32)
        return run

    norm_in()
    for c0 in range(0, cw, col):
        if c0 < pw:
            proj_pool(c0)()
        proj_glu(c0)()
        if c0 < pw:
            pool_group(c0 // gdim, POOL_WINDOWS[c0 // gdim])()
        for s in range(c0 // LANES, (c0 + col) // LANES):
            for r0 in range(0, tm, row_chunk):
                conv_chunk(r0, s)()
    norm_pool()
    norm_conv()
    pointwise()
    for c0 in range(0, x_ref.shape[1], col):
        out_cols(c0)()
    keep_halo(ubuf, POOL_HALO)
    keep_halo(ybuf, CONV_HALO)


def _mixer(x, g_mix, w_in, w_pool, pool_scale, w_dw, b_dw, ln_g, ln_b, w_pw, g_a, g_b, w_out):
    n, d = x.shape
    pw = w_pool.shape[0] * w_pool.shape[1]
    cw = w_pw.shape[0]
    tm = min(MIXER_TILE, n)
    assert cw % (pw // len(POOL_WINDOWS)) == 0 and pw <= cw
    row = pl.BlockSpec((tm, d), lambda j: (j, 0))
    return pl.pallas_call(
        _mixer_kernel,
        grid=(n // tm,),
        in_specs=[row, _resident(g_mix.shape), _resident(w_in.shape), _resident(w_pool.shape),
                  _resident(pool_scale.shape), _resident(w_dw.shape), _resident(b_dw.shape),
                  _resident(ln_g.shape), _resident(ln_b.shape), _resident(w_pw.shape),
                  _resident(g_a.shape), _resident(g_b.shape), _resident(w_out.shape)],
        out_specs=row,
        out_shape=jax.ShapeDtypeStruct((n, d), F32),
        scratch_shapes=[pltpu.VMEM((tm, d), BF16),
                        pltpu.VMEM((pw // LANES, TIME_PITCH * (tm + POOL_HALO), LANES), F32),
                        pltpu.VMEM((cw // LANES, TIME_PITCH * (tm + CONV_HALO), LANES), F32),
                        pltpu.VMEM((tm, cw), F32),
                        pltpu.VMEM((tm, cw), BF16),
                        pltpu.VMEM((tm, pw), F32),
                        pltpu.VMEM((tm, pw + cw), BF16)],
        compiler_params=pltpu.CompilerParams(dimension_semantics=("arbitrary",),
                                             vmem_limit_bytes=VMEM_LIMIT_BYTES),
        name="mixer",
    )(x, g_mix, w_in, w_pool, pool_scale, w_dw, b_dw, ln_g, ln_b, w_pw, g_a, g_b, w_out)


def _kv_kernel(mem_ref, g_ref, w_ref, o_ref):
    m = _rms(mem_ref[...], g_ref[...])
    o_ref[...] = jnp.dot(m.astype(BF16), w_ref[...].astype(BF16), preferred_element_type=F32).astype(BF16)


def _kv(mem, g_mem, w_kv):
    m, d = mem.shape
    n_out = w_kv.shape[1]
    bn = 512
    return pl.pallas_call(
        _kv_kernel,
        grid=(n_out // bn,),
        in_specs=[_resident((m, d)), _resident(g_mem.shape), pl.BlockSpec((d, bn), lambda j: (0, j))],
        out_specs=pl.BlockSpec((m, bn), lambda j: (0, j)),
        out_shape=jax.ShapeDtypeStruct((m, n_out), BF16),
        compiler_params=pltpu.CompilerParams(dimension_semantics=("arbitrary",),
                                             vmem_limit_bytes=VMEM_LIMIT_BYTES),
        name="kv",
    )(mem, g_mem, w_kv)


def _fold_kernel(wq_ref, wo_ref, kv_ref, a_ref, b_ref):
    hd = wq_ref.shape[1]
    d = wq_ref.shape[0]
    hh = pl.program_id(0)
    col = pl.multiple_of(hh * hd, hd)
    kh = kv_ref[:, pl.ds(col, hd)]
    vh = kv_ref[:, pl.ds(pl.multiple_of(d + hh * hd, hd), hd)]
    a_ref[...] = lax.dot_general(wq_ref[...].astype(BF16), kh, (((1,), (1,)), ((), ())),
                                 preferred_element_type=F32).astype(BF16)
    b_ref[...] = jnp.dot(vh, wo_ref[...].astype(BF16), preferred_element_type=F32).astype(BF16)


def _fold_memory(w_q, w_o, kv):
    d = w_q.shape[0]
    m = kv.shape[0]
    hd = d // XATTN_HEADS
    return pl.pallas_call(
        _fold_kernel,
        grid=(XATTN_HEADS,),
        in_specs=[pl.BlockSpec((d, hd), lambda h: (0, h)), pl.BlockSpec((hd, d), lambda h: (h, 0)),
                  _resident(kv.shape)],
        out_specs=[pl.BlockSpec((d, m), lambda h: (0, h)), pl.BlockSpec((m, d), lambda h: (h, 0))],
        out_shape=[jax.ShapeDtypeStruct((d, XATTN_HEADS * m), BF16), jax.ShapeDtypeStruct((XATTN_HEADS * m, d), BF16)],
        compiler_params=pltpu.CompilerParams(dimension_semantics=("arbitrary",),
                                             vmem_limit_bytes=VMEM_LIMIT_BYTES),
        name="fold",
    )(w_q, w_o, kv)


def _xattn_kernel(x_ref, gx_ref, a_ref, b_ref, gf_ref, wr_ref, o_ref, lg_ref):
    d = x_ref.shape[1]
    hd = d // XATTN_HEADS
    m = a_ref.shape[1] // XATTN_HEADS
    x = x_ref[...]
    h = _rms(x, gx_ref[...])
    s_all = jnp.dot(h.astype(BF16), a_ref[...], preferred_element_type=F32)
    ps = []
    for hh in range(XATTN_HEADS):
        s = s_all[:, hh * m:(hh + 1) * m] * (hd ** -0.5)
        e = jnp.exp(s - jnp.max(s, axis=-1, keepdims=True))
        ps.append((e / jnp.sum(e, axis=-1, keepdims=True)).astype(BF16))
    x2 = x + jnp.dot(jnp.concatenate(ps, axis=-1), b_ref[...], preferred_element_type=F32)
    _store_token_major(o_ref, x2)
    h3 = _rms(x2, gf_ref[...])
    lg_ref[...] = lax.dot_general(wr_ref[...], h3.astype(BF16), (((1,), (1,)), ((), ())),
                                  preferred_element_type=F32)


def _xattn(x, g_x, qk, vo, g_ffn, w_router_t):
    n, d = x.shape
    tm = min(XATTN_TILE, n)
    chunks = d // LANES
    row = pl.BlockSpec((tm, d), lambda i: (i, 0))
    return pl.pallas_call(
        _xattn_kernel,
        grid=(n // tm,),
        in_specs=[row, _resident(g_x.shape), _resident(qk.shape), _resident(vo.shape),
                  _resident(g_ffn.shape), _resident(w_router_t.shape)],
        out_specs=[pl.BlockSpec((tm * chunks, LANES), lambda i: (i, 0)),
                   pl.BlockSpec((LOGIT_ROWS, tm), lambda i: (0, i))],
        out_shape=[jax.ShapeDtypeStruct((n * chunks, LANES), F32), jax.ShapeDtypeStruct((LOGIT_ROWS, n), F32)],
        compiler_params=pltpu.CompilerParams(dimension_semantics=("arbitrary",),
                                             vmem_limit_bytes=VMEM_LIMIT_BYTES),
        name="xattn",
    )(x, g_x, qk, vo, g_ffn, w_router_t)


def _first_argmax(v, rows):
    m = jnp.max(v, axis=0, keepdims=True)
    idx = jnp.min(jnp.where(v == m, rows.astype(F32), float(v.shape[0])), axis=0, keepdims=True)
    return m, idx.astype(I32)


def _route_kernel(lg_ref, bias_ref, tri_ref, etri_ref, dest_ref, wtok_ref, blk_ref, bst_ref, eg_buf, rank_buf,
                  *, block):
    n = lg_ref.shape[1]
    t = tri_ref.shape[0]
    n_tiles = n // t
    epg = EXPERTS_PER_GROUP
    rows_g = lax.broadcasted_iota(I32, (N_GROUPS, t), 0)
    rows_e = lax.broadcasted_iota(I32, (epg, t), 0)
    rows_all = lax.broadcasted_iota(I32, (N_EXPERTS, t), 0)

    def tile_body(j, carry):
        sl = pl.ds(pl.multiple_of(j * t, t), t)
        le = lg_ref[0:N_EXPERTS, sl] + bias_ref[0:N_EXPERTS, :]
        lgp = lg_ref[N_EXPERTS:N_EXPERTS + N_GROUPS, sl] + bias_ref[N_EXPERTS:N_EXPERTS + N_GROUPS, :]
        eg_ = jnp.exp(lgp - jnp.max(lgp, axis=0, keepdims=True))
        pg = eg_ / jnp.sum(eg_, axis=0, keepdims=True)
        pg_top, g_idx = _first_argmax(pg, rows_g)
        le_sel = jnp.zeros((epg, t), F32)
        for g in range(N_GROUPS):
            le_sel = jnp.where(g_idx == g, le[g * epg:(g + 1) * epg, :], le_sel)
        ee = jnp.exp(le_sel - jnp.max(le_sel, axis=0, keepdims=True))
        pe = ee / jnp.sum(ee, axis=0, keepdims=True)
        p1, i1 = _first_argmax(pe, rows_e)
        p2, i2 = _first_argmax(jnp.where(rows_e == i1, -1.0, pe), rows_e)
        psum = p1 + p2
        w1 = pg_top * p1 / psum
        w2 = pg_top * p2 / psum
        e1 = g_idx * epg + i1
        e2 = g_idx * epg + i2
        oh1 = rows_all == e1
        oh2 = rows_all == e2
        onehot = jnp.where(oh1 | oh2, 1.0, 0.0)
        prefix = carry + jnp.dot(onehot.astype(BF16), tri_ref[...], preferred_element_type=F32)
        r1 = jnp.sum(jnp.where(oh1, prefix, 0.0), axis=0, keepdims=True)
        r2 = jnp.sum(jnp.where(oh2, prefix, 0.0), axis=0, keepdims=True)
        eg_buf[0:1, sl] = e1
        eg_buf[1:2, sl] = e2
        rank_buf[0:1, sl] = r1
        rank_buf[1:2, sl] = r2
        wtok_ref[0:1, sl] = w1
        wtok_ref[1:2, sl] = w2
        return carry + jnp.sum(onehot, axis=1, keepdims=True)

    counts = lax.fori_loop(0, n_tiles, tile_body, jnp.zeros((N_EXPERTS, 1), F32))
    wtok_ref[2:SUBLANES, :] = jnp.zeros((SUBLANES - 2, n), F32)

    nblk = jnp.floor((counts + (block - 1)) * (1.0 / block))
    nblk_b = jnp.broadcast_to(nblk, (N_EXPERTS, LANES)).astype(BF16)
    bstart = jnp.dot(etri_ref[...], nblk_b, preferred_element_type=F32)[:, 0:1]
    bend = bstart + nblk
    pstart = bstart * block

    def dest_body(j, _):
        sl = pl.ds(pl.multiple_of(j * t, t), t)
        for k in range(TOP_K):
            oh = rows_all == eg_buf[k:k + 1, sl]
            base = jnp.sum(jnp.where(oh, pstart, 0.0), axis=0, keepdims=True)
            dest_ref[k:k + 1, sl] = (base + rank_buf[k:k + 1, sl]).astype(I32)
        return 0

    lax.fori_loop(0, n_tiles, dest_body, 0)
    dest_ref[2:SUBLANES, :] = jnp.zeros((SUBLANES - 2, n), I32)

    nbp = blk_ref.shape[1]
    bidx = lax.broadcasted_iota(I32, (N_EXPERTS, nbp), 1).astype(F32)
    be = jnp.sum(jnp.where(bend <= bidx, 1.0, 0.0), axis=0, keepdims=True)
    n_used = jnp.max(bend, axis=0, keepdims=True)
    blk_ref[0:1, :] = be.astype(I32)
    blk_ref[1:2, :] = jnp.broadcast_to(n_used, (1, nbp)).astype(I32)
    blk_ref[2:SUBLANES, :] = jnp.zeros((SUBLANES - 2, nbp), I32)
    lane = lax.broadcasted_iota(I32, bst_ref.shape, 1)
    bst_ref[...] = jnp.where(lane == 1, counts, bstart).astype(I32)
    eg_buf[2:SUBLANES, :] = jnp.zeros((SUBLANES - 2, n), I32)


def _route(logits_t, bias, block, nb):
    n = logits_t.shape[1]
    t = min(ROUTE_TILE, n)
    nbp = pl.cdiv(nb, LANES) * LANES
    assert nbp <= 256, "block counts must stay exactly representable in bf16"
    tri = (lax.broadcasted_iota(I32, (t, t), 0) < lax.broadcasted_iota(I32, (t, t), 1)).astype(BF16)
    etri = (lax.broadcasted_iota(I32, (N_EXPERTS, N_EXPERTS), 1)
            < lax.broadcasted_iota(I32, (N_EXPERTS, N_EXPERTS), 0)).astype(BF16)
    return pl.pallas_call(
        functools.partial(_route_kernel, block=block),
        grid=(1,),
        in_specs=[pl.BlockSpec((ROUTE_ROWS, n), lambda i: (0, 0)), _resident(bias.shape), _resident(tri.shape),
                  _resident(etri.shape)],
        out_specs=[pl.BlockSpec((SUBLANES, n), lambda i: (0, 0)), pl.BlockSpec((SUBLANES, n), lambda i: (0, 0)),
                   pl.BlockSpec((SUBLANES, nbp), lambda i: (0, 0)),
                   pl.BlockSpec((N_EXPERTS, LANES), lambda i: (0, 0)),
                   pl.BlockSpec((SUBLANES, n), lambda i: (0, 0))],
        out_shape=[jax.ShapeDtypeStruct((SUBLANES, n), I32), jax.ShapeDtypeStruct((SUBLANES, n), F32),
                   jax.ShapeDtypeStruct((SUBLANES, nbp), I32), jax.ShapeDtypeStruct((N_EXPERTS, LANES), I32),
                   jax.ShapeDtypeStruct((SUBLANES, n), I32)],
        scratch_shapes=[pltpu.VMEM((SUBLANES, n), F32)],
        compiler_params=pltpu.CompilerParams(dimension_semantics=("arbitrary",),
                                             vmem_limit_bytes=VMEM_LIMIT_BYTES),
        name="route",
    )(logits_t, bias, tri, etri)


def _start_token_gather(tok, src_hbm, dst, sem, row0, chunks):
    pltpu.make_async_copy(src_hbm.at[pl.ds(pl.multiple_of(tok * chunks, chunks), chunks)],
                          dst.at[pl.ds(row0, chunks)], sem).start()


def _gather_tokens(token_of, src_hbm, dst, sem, count, chunks, unrolled):
    pitch = _gather_pitch(chunks)
    if unrolled:
        for r in range(count):
            _start_token_gather(token_of(r), src_hbm, dst, sem, r * pitch, chunks)
    else:
        def body(r, _):
            _start_token_gather(token_of(r), src_hbm, dst, sem, pl.multiple_of(r * pitch, SUBLANES), chunks)
            return 0
        lax.fori_loop(0, count, body, 0, unroll=8)


def _wait_token_gathers(src_hbm, dst, sem, count, chunks):
    pltpu.make_async_copy(src_hbm.at[pl.ds(0, count * chunks)], dst.at[pl.ds(0, count * chunks)], sem).wait()


def _experts_kernel(bnd_ref, base_ref, tok_ref, x_hbm, gf_ref, wg_hbm, wu_hbm, wd_hbm, y_hbm,
                    xg, gsem, obuf, osem, wg32, wu32, wd32, wsem, wg16, wu16, wd16, *, rows, n_blocks):
    e = pl.program_id(0)
    ne = pl.num_programs(0)
    block_rows = obuf.shape[1]
    chunks = block_rows // rows
    pitch = _gather_pitch(chunks)
    n_used = bnd_ref[ne]

    def gather(b, slot, unrolled):
        base = base_ref[b]
        _gather_tokens(lambda r: tok_ref[base + r], x_hbm, xg.at[slot], gsem.at[slot], rows, chunks, unrolled)

    def out_copy(b, slot):
        dst = y_hbm.at[pl.ds(pl.multiple_of(b * block_rows, block_rows), block_rows)]
        return pltpu.make_async_copy(obuf.at[slot], dst, osem.at[slot])

    def weight_copies(expert, slot):
        return [pltpu.make_async_copy(hbm.at[expert], buf.at[slot], wsem.at[slot])
                for hbm, buf in ((wg_hbm, wg32), (wu_hbm, wu32), (wd_hbm, wd32))]

    @pl.when(e == 0)
    def _():
        for cp in weight_copies(0, 0):
            cp.start(priority=1)
        gather(0, 0, unrolled=False)

    @pl.when(e + 1 < ne)
    def _():
        for cp in weight_copies(e + 1, (e + 1) % 2):
            cp.start(priority=1)

    for cp in weight_copies(e, e % 2):
        cp.wait()

    @pl.when(bnd_ref[e + 1] > bnd_ref[e])
    def _():
        wg16[...] = wg32[e % 2].astype(BF16)
        wu16[...] = wu32[e % 2].astype(BF16)
        wd16[...] = wd32[e % 2].astype(BF16)

    def block_body(b, carry):
        slot = b % 2

        @pl.when(b >= 2)
        def _():
            out_copy(b - 2, slot).wait()

        _wait_token_gathers(x_hbm, xg.at[slot], gsem.at[slot], rows, chunks)
        xb = _load_token_major(xg.at[slot], rows, chunks, pitch)
        gather(jnp.minimum(b + 1, n_blocks - 1), 1 - slot, unrolled=True)
        h = _rms(xb, gf_ref[...]).astype(BF16)
        hg = jnp.dot(h, wg16[...], preferred_element_type=F32)
        hu = jnp.dot(h, wu16[...], preferred_element_type=F32)
        hid = hg * (1.0 / (1.0 + jnp.exp(-hg))) * hu
        _store_token_major(obuf.at[slot], jnp.dot(hid.astype(BF16), wd16[...], preferred_element_type=F32))
        out_copy(b, slot).start()
        return carry

    lax.fori_loop(bnd_ref[e], bnd_ref[e + 1], block_body, 0)

    @pl.when(e == ne - 1)
    def _():
        _wait_token_gathers(x_hbm, xg.at[n_used % 2], gsem.at[n_used % 2], rows, chunks)

        @pl.when(n_used >= 2)
        def _():
            out_copy(n_used - 2, n_used % 2).wait()

        out_copy(n_used - 1, (n_used - 1) % 2).wait()
        obuf[0] = jnp.zeros(obuf.shape[1:], F32)

        def fill(b, carry):
            out_copy(b, 0).start()
            return carry

        def drain(b, carry):
            out_copy(b, 0).wait()
            return carry

        lax.fori_loop(n_used, n_blocks, fill, 0)
        lax.fori_loop(n_used, n_blocks, drain, 0)


def _experts(bounds, block_base, sorted_tok, x2c, g_ffn, w_gate, w_up, w_down, block):
    n_blocks = block_base.shape[0]
    ne, d, de = w_gate.shape
    chunks = d // LANES
    grid_spec = pltpu.PrefetchScalarGridSpec(
        num_scalar_prefetch=3,
        grid=(ne,),
        in_specs=[pl.BlockSpec(memory_space=pl.ANY),
                  _resident(g_ffn.shape),
                  pl.BlockSpec(memory_space=pl.ANY),
                  pl.BlockSpec(memory_space=pl.ANY),
                  pl.BlockSpec(memory_space=pl.ANY)],
        out_specs=pl.BlockSpec(memory_space=pl.ANY),
        scratch_shapes=[pltpu.VMEM((2, block * _gather_pitch(chunks), LANES), F32), pltpu.SemaphoreType.DMA((2,)),
                        pltpu.VMEM((2, block * chunks, LANES), F32), pltpu.SemaphoreType.DMA((2,)),
                        pltpu.VMEM((2, d, de), F32), pltpu.VMEM((2, d, de), F32), pltpu.VMEM((2, de, d), F32),
                        pltpu.SemaphoreType.DMA((2,)),
                        pltpu.VMEM((d, de), BF16), pltpu.VMEM((d, de), BF16), pltpu.VMEM((de, d), BF16)],
    )
    return pl.pallas_call(
        functools.partial(_experts_kernel, rows=block, n_blocks=n_blocks),
        grid_spec=grid_spec,
        out_shape=jax.ShapeDtypeStruct((n_blocks * block * chunks, LANES), F32),
        compiler_params=pltpu.CompilerParams(dimension_semantics=("arbitrary",),
                                             vmem_limit_bytes=VMEM_LIMIT_BYTES),
        name="experts",
    )(bounds, block_base, sorted_tok, x2c, g_ffn, w_gate, w_up, w_down)


def _combine_kernel(d_ref, dn_ref, x_ref, w_ref, y_hbm, g_ref, o_ref, ybuf, sems):
    i = pl.program_id(0)
    nt = pl.num_programs(0)
    tm, d = o_ref.shape
    chunks = d // LANES
    slot = i % 2

    def expert_rows(k):
        return _load_token_major(ybuf.at[slot, k], tm, chunks, _gather_pitch(chunks))

    def start(idx_ref, s):
        for k in range(TOP_K):
            _gather_tokens(lambda r, k=k: idx_ref[0, 0, k * tm + r], y_hbm, ybuf.at[s, k], sems.at[s], tm, chunks,
                           unrolled=False)

    def wait(s):
        for k in range(TOP_K):
            _wait_token_gathers(y_hbm, ybuf.at[s, k], sems.at[s], tm, chunks)

    @pl.when(i == 0)
    def _():
        start(d_ref, 0)

    start(dn_ref, 1 - slot)
    wait(slot)
    w = w_ref[...]
    y = _load_token_major(x_ref, tm, chunks, chunks) + (expert_rows(0) * w[:, 0:1] + expert_rows(1) * w[:, 1:2])
    o_ref[...] = _rms(y, g_ref[...])

    @pl.when(i == nt - 1)
    def _():
        wait(1 - slot)


def _combine(dest_tiles, x2c, w_tok, ysc, g_final):
    d = g_final.shape[1]
    chunks = d // LANES
    n = x2c.shape[0] // chunks
    tm = dest_tiles.shape[2] // TOP_K
    nt = n // tm
    smem_blk = lambda f: pl.BlockSpec((1, 1, TOP_K * tm), f, memory_space=pltpu.SMEM)
    return pl.pallas_call(
        _combine_kernel,
        grid=(nt,),
        in_specs=[smem_blk(lambda i: (i, 0, 0)), smem_blk(lambda i: (jnp.minimum(i + 1, nt - 1), 0, 0)),
                  pl.BlockSpec((tm * chunks, LANES), lambda i: (i, 0)),
                  pl.BlockSpec((tm, SUBLANES), lambda i: (i, 0)), pl.BlockSpec(memory_space=pl.ANY),
                  _resident(g_final.shape)],
        out_specs=pl.BlockSpec((tm, d), lambda i: (i, 0)),
        out_shape=jax.ShapeDtypeStruct((n, d), F32),
        scratch_shapes=[pltpu.VMEM((2, TOP_K, tm * _gather_pitch(chunks), LANES), F32),
                        pltpu.SemaphoreType.DMA((2,))],
        compiler_params=pltpu.CompilerParams(dimension_semantics=("arbitrary",),
                                             vmem_limit_bytes=VMEM_LIMIT_BYTES),
        name="combine",
    )(dest_tiles, dest_tiles, x2c, w_tok, ysc, g_final)


def _layer(x, mem, norm_mix_g, w_in, w_pool, pool_scale, w_dw, b_dw, conv_ln_g, conv_ln_b, w_conv_pw,
           out_norm_a_g, out_norm_b_g, w_out, norm_xattn_g, norm_mem_g, w_q_mem, w_kv_mem, w_o_mem, norm_ffn_g,
           w_router_group, b_router_group, w_router_expert, b_router_expert, w_exp_gate, w_exp_up, w_exp_down,
           out_gain):
    n, d = x.shape
    r2 = lambda v: v.reshape(1, -1)
    cw = w_conv_pw.shape[0]
    x1 = _mixer(x, r2(norm_mix_g), w_in.astype(BF16), w_pool.astype(BF16), r2(pool_scale),
                w_dw.reshape(CONV_KERNEL, cw), r2(b_dw),
                r2(conv_ln_g), r2(conv_ln_b), w_conv_pw.astype(BF16), r2(out_norm_a_g), r2(out_norm_b_g),
                w_out.astype(BF16))

    kv = _kv(mem, r2(norm_mem_g), w_kv_mem)
    w_router_t = jnp.zeros((LOGIT_ROWS, d), F32)
    w_router_t = w_router_t.at[:N_EXPERTS].set(w_router_expert.T).at[N_EXPERTS:N_EXPERTS + N_GROUPS].set(
        w_router_group.T).astype(BF16)
    qk, vo = _fold_memory(w_q_mem, w_o_mem, kv)
    x2, logits_t = _xattn(x1, r2(norm_xattn_g), qk, vo, r2(norm_ffn_g), w_router_t)

    block = MOE_BLOCK
    nb = pl.cdiv(n * TOP_K, block) + N_EXPERTS
    bias = jnp.zeros((ROUTE_ROWS, 1), F32)
    bias = bias.at[:N_EXPERTS, 0].set(b_router_expert).at[N_EXPERTS:N_EXPERTS + N_GROUPS, 0].set(b_router_group)
    dest, w_tok, blk, bst, eg = _route(logits_t, bias, block, nb)
    dest = dest[:TOP_K]
    n_used = blk[1, 0]
    bounds = jnp.concatenate([bst[:, 0], blk[1, :1]])
    keys = jnp.sort((eg[:TOP_K] * n + jnp.arange(n, dtype=I32)).reshape(-1))
    sorted_tok = jnp.concatenate([keys % n, jnp.arange(block, dtype=I32)])
    first_row = jnp.cumsum(bst[:, 1]) - bst[:, 1]
    bidx = jnp.arange(nb, dtype=I32)
    block_e = jnp.minimum(blk[0, :nb], N_EXPERTS - 1)
    block_base = jnp.where(bidx < n_used, first_row[block_e] + (bidx - bst[block_e, 0]) * block, 0)

    ys = _experts(bounds, block_base, sorted_tok, x2, r2(norm_ffn_g), w_exp_gate, w_exp_up, w_exp_down, block)

    tm = min(COMBINE_TILE, n)
    dest_tiles = dest.reshape(TOP_K, n // tm, tm).transpose(1, 0, 2).reshape(n // tm, 1, TOP_K * tm)
    return _combine(dest_tiles, x2, w_tok.T, ys, r2(out_gain))


def kernel(x, mem, norm_mix_g, w_in, w_pool, pool_scale, w_dw, b_dw, conv_ln_g, conv_ln_b, w_conv_pw, out_norm_a_g,
           out_norm_b_g, w_out, norm_xattn_g, norm_mem_g, w_q_mem, w_kv_mem, w_o_mem, norm_ffn_g, w_router_group,
           b_router_group, w_router_expert, b_router_expert, w_exp_gate, w_exp_up, w_exp_down, final_norm_g):
    assert x.shape[0] == 1 and mem.shape[0] == 1 and norm_mix_g.shape[0] == 1
    out = _layer(x[0], mem[0], norm_mix_g[0], w_in[0], w_pool[0], pool_scale[0], w_dw[0], b_dw[0], conv_ln_g[0],
                 conv_ln_b[0], w_conv_pw[0], out_norm_a_g[0], out_norm_b_g[0], w_out[0], norm_xattn_g[0],
                 norm_mem_g[0], w_q_mem[0], w_kv_mem[0], w_o_mem[0], norm_ffn_g[0], w_router_group[0],
                 b_router_group[0], w_router_expert[0], b_router_expert[0], w_exp_gate[0], w_exp_up[0],
                 w_exp_down[0], final_norm_g)
    return out[None]
```

```python
import functools

import jax
import jax.numpy as jnp
from jax import lax
from jax.experimental import pallas as pl
from jax.experimental.pallas import tpu as pltpu

F32 = jnp.float32
BF16 = jnp.bfloat16
I32 = jnp.int32

EPS = 1e-6
POOL_WINDOWS = (2, 4, 8, 16)
CONV_KERNEL = 31
XATTN_HEADS = 4
N_GROUPS = 4
EXPERTS_PER_GROUP = 8
N_EXPERTS = N_GROUPS * EXPERTS_PER_GROUP
TOP_K = 2

SUBLANES = 8
LANES = 128
VMEM_LIMIT_BYTES = 56 * 1024 * 1024

POOL_HALO = 16
TIME_PITCH = 2
CONV_HALO = 32
MIXER_TILE = 512
XATTN_TILE = 1024
ROUTE_TILE = 512
MOE_BLOCK = 256
COMBINE_TILE = 256
LOGIT_ROWS = 128
ROUTE_ROWS = 40


def _rms(x, g):
    return x * lax.rsqrt(jnp.mean(x * x, axis=-1, keepdims=True) + EPS) * g


def _resident(shape):
    nd = len(shape)
    return pl.BlockSpec(shape, lambda *_: (0,) * nd, pipeline_mode=pl.Buffered(1))


def _store_token_major(ref, val):
    t, d = val.shape
    chunks = d // LANES
    for c in range(chunks):
        ref[pl.ds(c, t, stride=chunks), :] = val[:, c * LANES:(c + 1) * LANES]


def _load_token_major(ref, t, chunks, pitch):
    return jnp.concatenate([ref[pl.ds(c, t, stride=pitch), :] for c in range(chunks)], axis=-1)


def _gather_pitch(chunks):
    tiles = pl.cdiv(chunks, SUBLANES)
    return SUBLANES * (tiles + 1 - tiles % 2)


def _mixer_kernel(x_ref, gmix_ref, win_ref, wpool_ref, pscale_ref, wdw_ref, bdw_ref, lng_ref, lnb_ref,
                  wpw_ref, ga_ref, gb_ref, wout_ref, o_ref, hbuf, ubuf, ybuf, cbuf, cbf, ya_new, ycat):
    j = pl.program_id(0)
    tm = x_ref.shape[0]
    pw = ubuf.shape[0] * LANES
    cw = ybuf.shape[0] * LANES
    gdim = pw // len(POOL_WINDOWS)
    col = gdim

    @pl.when(j == 0)
    def _():
        ubuf[:, 0:TIME_PITCH * POOL_HALO, :] = jnp.zeros((pw // LANES, TIME_PITCH * POOL_HALO, LANES), F32)
        ybuf[:, 0:TIME_PITCH * CONV_HALO, :] = jnp.zeros((cw // LANES, TIME_PITCH * CONV_HALO, LANES), F32)

    def frames(buf, slab, r, count):
        return buf[slab, pl.ds(TIME_PITCH * r, count, stride=TIME_PITCH), :]

    def put_frames(buf, slab0, r, val):
        for s in range(val.shape[1] // LANES):
            buf[slab0 + s, pl.ds(TIME_PITCH * r, val.shape[0], stride=TIME_PITCH), :] = val[:, s * LANES:(s + 1) * LANES]

    def keep_halo(buf, halo):
        buf[:, 0:TIME_PITCH * halo, :] = buf[:, TIME_PITCH * tm:TIME_PITCH * (tm + halo), :]

    def norm_in():
        hbuf[...] = _rms(x_ref[...], gmix_ref[...]).astype(BF16)

    def proj_pool(c0):
        def run():
            u = jnp.dot(hbuf[...], win_ref[:, c0:c0 + col], preferred_element_type=F32)
            put_frames(ubuf, c0 // LANES, POOL_HALO, u)
        return run

    def proj_glu(c0):
        def run():
            a = jnp.dot(hbuf[...], win_ref[:, pw + c0:pw + c0 + col], preferred_element_type=F32)
            g = jnp.dot(hbuf[...], win_ref[:, pw + cw + c0:pw + cw + c0 + col], preferred_element_type=F32)
            put_frames(ybuf, c0 // LANES, CONV_HALO, a * (1.0 / (1.0 + jnp.exp(-g))))
        return run

    def pool_group(g, w):
        def run():
            pos = j * tm + lax.broadcasted_iota(I32, (tm, 1), 0)
            cols, toks = [], []
            for s in range(g * gdim // LANES, (g + 1) * gdim // LANES):
                tok = frames(ubuf, s, POOL_HALO, tm)
                wsum = tok
                for k in range(1, w):
                    wsum = wsum + frames(ubuf, s, POOL_HALO - k, tm)
                cols.append(wsum)
                toks.append(tok)
            cnt = jnp.minimum(pos + 1, w).astype(F32)
            d = jnp.concatenate(cols, axis=-1) / cnt - jnp.concatenate(toks, axis=-1)
            z = jnp.dot(d.astype(BF16), wpool_ref[g], preferred_element_type=F32)
            ya_new[:, g * gdim:(g + 1) * gdim] = z * pscale_ref[:, g * gdim:(g + 1) * gdim]
        return run

    def norm_pool():
        ya_new[...] = _rms(ya_new[...], ga_ref[...])

    first = CONV_HALO - (CONV_KERNEL - 1)
    row_chunk = 64

    def conv_chunk(r0, s):
        def run():
            acc = None
            for k in range(CONV_KERNEL):
                term = frames(ybuf, s, first + r0 + k, row_chunk) * wdw_ref[k:k + 1, s * LANES:(s + 1) * LANES]
                acc = term if acc is None else acc + term
            cbuf[r0:r0 + row_chunk, s * LANES:(s + 1) * LANES] = acc
        return run

    def norm_conv():
        c = cbuf[...] + bdw_ref[...]
        mu = jnp.mean(c, axis=-1, keepdims=True)
        var = jnp.mean(jnp.square(c - mu), axis=-1, keepdims=True)
        c = (c - mu) * lax.rsqrt(var + EPS) * lng_ref[...] + lnb_ref[...]
        cbf[...] = (c * (1.0 / (1.0 + jnp.exp(-c)))).astype(BF16)

    def pointwise():
        yb = jnp.dot(cbf[...], wpw_ref[...], preferred_element_type=F32)
        ycat[:, 0:pw] = ya_new[...].astype(BF16)
        ycat[:, pw:pw + cw] = _rms(yb, gb_ref[...]).astype(BF16)

    def out_cols(c0):
        def run():
            o_ref[:, c0:c0 + col] = x_ref[:, c0:c0 + col] + jnp.dot(
                ycat[...], wout_ref[:, c0:c0 + col], preferred_element_type=F32)
        return run

    norm_in()
    for c0 in range(0, cw, col):
        if c0 < pw:
            proj_pool(c0)()
        proj_glu(c0)()
        if c0 < pw:
            pool_group(c0 // gdim, POOL_WINDOWS[c0 // gdim])()
        for s in range(c0 // LANES, (c0 + col) // LANES):
            for r0 in range(0, tm, row_chunk):
                conv_chunk(r0, s)()
    norm_pool()
    norm_conv()
    pointwise()
    for c0 in range(0, x_ref.shape[1], col):
        out_cols(c0)()
    keep_halo(ubuf, POOL_HALO)
    keep_halo(ybuf, CONV_HALO)


def _mixer(x, g_mix, w_in, w_pool, pool_scale, w_dw, b_dw, ln_g, ln_b, w_pw, g_a, g_b, w_out):
    n, d = x.shape
    pw = w_pool.shape[0] * w_pool.shape[1]
    cw = w_pw.shape[0]
    tm = min(MIXER_TILE, n)
    assert cw % (pw // len(POOL_WINDOWS)) == 0 and pw <= cw
    row = pl.BlockSpec((tm, d), lambda j: (j, 0))
    return pl.pallas_call(
        _mixer_kernel,
        grid=(n // tm,),
        in_specs=[row, _resident(g_mix.shape), _resident(w_in.shape), _resident(w_pool.shape),
                  _resident(pool_scale.shape), _resident(w_dw.shape), _resident(b_dw.shape),
                  _resident(ln_g.shape), _resident(ln_b.shape), _resident(w_pw.shape),
                  _resident(g_a.shape), _resident(g_b.shape), _resident(w_out.shape)],
        out_specs=row,
        out_shape=jax.ShapeDtypeStruct((n, d), F32),
        scratch_shapes=[pltpu.VMEM((tm, d), BF16),
                        pltpu.VMEM((pw // LANES, TIME_PITCH * (tm + POOL_HALO), LANES), F32),
                        pltpu.VMEM((cw // LANES, TIME_PITCH * (tm + CONV_HALO), LANES), F32),
                        pltpu.VMEM((tm, cw), F32),
                        pltpu.VMEM((tm, cw), BF16),
                        pltpu.VMEM((tm, pw), F32),
                        pltpu.VMEM((tm, pw + cw), BF16)],
        compiler_params=pltpu.CompilerParams(dimension_semantics=("arbitrary",),
                                             vmem_limit_bytes=VMEM_LIMIT_BYTES),
        name="mixer",
    )(x, g_mix, w_in, w_pool, pool_scale, w_dw, b_dw, ln_g, ln_b, w_pw, g_a, g_b, w_out)


def _kv_kernel(mem_ref, g_ref, w_ref, o_ref):
    m = _rms(mem_ref[...], g_ref[...])
    o_ref[...] = jnp.dot(m.astype(BF16), w_ref[...].astype(BF16), preferred_element_type=F32).astype(BF16)


def _kv(mem, g_mem, w_kv):
    m, d = mem.shape
    n_out = w_kv.shape[1]
    bn = 512
    return pl.pallas_call(
        _kv_kernel,
        grid=(n_out // bn,),
        in_specs=[_resident((m, d)), _resident(g_mem.shape), pl.BlockSpec((d, bn), lambda j: (0, j))],
        out_specs=pl.BlockSpec((m, bn), lambda j: (0, j)),
        out_shape=jax.ShapeDtypeStruct((m, n_out), BF16),
        compiler_params=pltpu.CompilerParams(dimension_semantics=("arbitrary",),
                                             vmem_limit_bytes=VMEM_LIMIT_BYTES),
        name="kv",
    )(mem, g_mem, w_kv)


def _fold_kernel(wq_ref, wo_ref, kv_ref, a_ref, b_ref):
    hd = wq_ref.shape[1]
    d = wq_ref.shape[0]
    hh = pl.program_id(0)
    col = pl.multiple_of(hh * hd, hd)
    kh = kv_ref[:, pl.ds(col, hd)]
    vh = kv_ref[:, pl.ds(pl.multiple_of(d + hh * hd, hd), hd)]
    a_ref[...] = lax.dot_general(wq_ref[...].astype(BF16), kh, (((1,), (1,)), ((), ())),
                                 preferred_element_type=F32).astype(BF16)
    b_ref[...] = jnp.dot(vh, wo_ref[...].astype(BF16), preferred_element_type=F32).astype(BF16)


def _fold_memory(w_q, w_o, kv):
    d = w_q.shape[0]
    m = kv.shape[0]
    hd = d // XATTN_HEADS
    return pl.pallas_call(
        _fold_kernel,
        grid=(XATTN_HEADS,),
        in_specs=[pl.BlockSpec((d, hd), lambda h: (0, h)), pl.BlockSpec((hd, d), lambda h: (h, 0)),
                  _resident(kv.shape)],
        out_specs=[pl.BlockSpec((d, m), lambda h: (0, h)), pl.BlockSpec((m, d), lambda h: (h, 0))],
        out_shape=[jax.ShapeDtypeStruct((d, XATTN_HEADS * m), BF16), jax.ShapeDtypeStruct((XATTN_HEADS * m, d), BF16)],
        compiler_params=pltpu.CompilerParams(dimension_semantics=("arbitrary",),
                                             vmem_limit_bytes=VMEM_LIMIT_BYTES),
        name="fold",
    )(w_q, w_o, kv)


def _xattn_kernel(x_ref, gx_ref, a_ref, b_ref, gf_ref, wr_ref, o_ref, lg_ref):
    d = x_ref.shape[1]
    hd = d // XATTN_HEADS
    m = a_ref.shape[1] // XATTN_HEADS
    x = x_ref[...]
    h = _rms(x, gx_ref[...])
    s_all = jnp.dot(h.astype(BF16), a_ref[...], preferred_element_type=F32)
    ps = []
    for hh in range(XATTN_HEADS):
        s = s_all[:, hh * m:(hh + 1) * m] * (hd ** -0.5)
        e = jnp.exp(s - jnp.max(s, axis=-1, keepdims=True))
        ps.append((e / jnp.sum(e, axis=-1, keepdims=True)).astype(BF16))
    x2 = x + jnp.dot(jnp.concatenate(ps, axis=-1), b_ref[...], preferred_element_type=F32)
    _store_token_major(o_ref, x2)
    h3 = _rms(x2, gf_ref[...])
    lg_ref[...] = lax.dot_general(wr_ref[...], h3.astype(BF16), (((1,), (1,)), ((), ())),
                                  preferred_element_type=F32)


def _xattn(x, g_x, qk, vo, g_ffn, w_router_t):
    n, d = x.shape
    tm = min(XATTN_TILE, n)
    chunks = d // LANES
    row = pl.BlockSpec((tm, d), lambda i: (i, 0))
    return pl.pallas_call(
        _xattn_kernel,
        grid=(n // tm,),
        in_specs=[row, _resident(g_x.shape), _resident(qk.shape), _resident(vo.shape),
                  _resident(g_ffn.shape), _resident(w_router_t.shape)],
        out_specs=[pl.BlockSpec((tm * chunks, LANES), lambda i: (i, 0)),
                   pl.BlockSpec((LOGIT_ROWS, tm), lambda i: (0, i))],
        out_shape=[jax.ShapeDtypeStruct((n * chunks, LANES), F32), jax.ShapeDtypeStruct((LOGIT_ROWS, n), F32)],
        compiler_params=pltpu.CompilerParams(dimension_semantics=("arbitrary",),
                                             vmem_limit_bytes=VMEM_LIMIT_BYTES),
        name="xattn",
    )(x, g_x, qk, vo, g_ffn, w_router_t)


def _first_argmax(v, rows):
    m = jnp.max(v, axis=0, keepdims=True)
    idx = jnp.min(jnp.where(v == m, rows.astype(F32), float(v.shape[0])), axis=0, keepdims=True)
    return m, idx.astype(I32)


def _route_kernel(lg_ref, bias_ref, tri_ref, etri_ref, dest_ref, wtok_ref, blk_ref, bst_ref, eg_buf, rank_buf,
                  *, block):
    n = lg_ref.shape[1]
    t = tri_ref.shape[0]
    n_tiles = n // t
    epg = EXPERTS_PER_GROUP
    rows_g = lax.broadcasted_iota(I32, (N_GROUPS, t), 0)
    rows_e = lax.broadcasted_iota(I32, (epg, t), 0)
    rows_all = lax.broadcasted_iota(I32, (N_EXPERTS, t), 0)

    def tile_body(j, carry):
        sl = pl.ds(pl.multiple_of(j * t, t), t)
        le = lg_ref[0:N_EXPERTS, sl] + bias_ref[0:N_EXPERTS, :]
        lgp = lg_ref[N_EXPERTS:N_EXPERTS + N_GROUPS, sl] + bias_ref[N_EXPERTS:N_EXPERTS + N_GROUPS, :]
        eg_ = jnp.exp(lgp - jnp.max(lgp, axis=0, keepdims=True))
        pg = eg_ / jnp.sum(eg_, axis=0, keepdims=True)
        pg_top, g_idx = _first_argmax(pg, rows_g)
        le_sel = jnp.zeros((epg, t), F32)
        for g in range(N_GROUPS):
            le_sel = jnp.where(g_idx == g, le[g * epg:(g + 1) * epg, :], le_sel)
        ee = jnp.exp(le_sel - jnp.max(le_sel, axis=0, keepdims=True))
        pe = ee / jnp.sum(ee, axis=0, keepdims=True)
        p1, i1 = _first_argmax(pe, rows_e)
        p2, i2 = _first_argmax(jnp.where(rows_e == i1, -1.0, pe), rows_e)
        psum = p1 + p2
        w1 = pg_top * p1 / psum
        w2 = pg_top * p2 / psum
        e1 = g_idx * epg + i1
        e2 = g_idx * epg + i2
        oh1 = rows_all == e1
        oh2 = rows_all == e2
        onehot = jnp.where(oh1 | oh2, 1.0, 0.0)
        prefix = carry + jnp.dot(onehot.astype(BF16), tri_ref[...], preferred_element_type=F32)
        r1 = jnp.sum(jnp.where(oh1, prefix, 0.0), axis=0, keepdims=True)
        r2 = jnp.sum(jnp.where(oh2, prefix, 0.0), axis=0, keepdims=True)
        eg_buf[0:1, sl] = e1
        eg_buf[1:2, sl] = e2
        rank_buf[0:1, sl] = r1
        rank_buf[1:2, sl] = r2
        wtok_ref[0:1, sl] = w1
        wtok_ref[1:2, sl] = w2
        return carry + jnp.sum(onehot, axis=1, keepdims=True)

    counts = lax.fori_loop(0, n_tiles, tile_body, jnp.zeros((N_EXPERTS, 1), F32))
    wtok_ref[2:SUBLANES, :] = jnp.zeros((SUBLANES - 2, n), F32)

    nblk = jnp.floor((counts + (block - 1)) * (1.0 / block))
    nblk_b = jnp.broadcast_to(nblk, (N_EXPERTS, LANES)).astype(BF16)
    bstart = jnp.dot(etri_ref[...], nblk_b, preferred_element_type=F32)[:, 0:1]
    bend = bstart + nblk
    pstart = bstart * block

    def dest_body(j, _):
        sl = pl.ds(pl.multiple_of(j * t, t), t)
        for k in range(TOP_K):
            oh = rows_all == eg_buf[k:k + 1, sl]
            base = jnp.sum(jnp.where(oh, pstart, 0.0), axis=0, keepdims=True)
            dest_ref[k:k + 1, sl] = (base + rank_buf[k:k + 1, sl]).astype(I32)
        return 0

    lax.fori_loop(0, n_tiles, dest_body, 0)
    dest_ref[2:SUBLANES, :] = jnp.zeros((SUBLANES - 2, n), I32)

    nbp = blk_ref.shape[1]
    bidx = lax.broadcasted_iota(I32, (N_EXPERTS, nbp), 1).astype(F32)
    be = jnp.sum(jnp.where(bend <= bidx, 1.0, 0.0), axis=0, keepdims=True)
    n_used = jnp.max(bend, axis=0, keepdims=True)
    blk_ref[0:1, :] = be.astype(I32)
    blk_ref[1:2, :] = jnp.broadcast_to(n_used, (1, nbp)).astype(I32)
    blk_ref[2:SUBLANES, :] = jnp.zeros((SUBLANES - 2, nbp), I32)
    lane = lax.broadcasted_iota(I32, bst_ref.shape, 1)
    bst_ref[...] = jnp.where(lane == 1, counts, bstart).astype(I32)
    eg_buf[2:SUBLANES, :] = jnp.zeros((SUBLANES - 2, n), I32)


def _route(logits_t, bias, block, nb):
    n = logits_t.shape[1]
    t = min(ROUTE_TILE, n)
    nbp = pl.cdiv(nb, LANES) * LANES
    assert nbp <= 256, "block counts must stay exactly representable in bf16"
    tri = (lax.broadcasted_iota(I32, (t, t), 0) < lax.broadcasted_iota(I32, (t, t), 1)).astype(BF16)
    etri = (lax.broadcasted_iota(I32, (N_EXPERTS, N_EXPERTS), 1)
            < lax.broadcasted_iota(I32, (N_EXPERTS, N_EXPERTS), 0)).astype(BF16)
    return pl.pallas_call(
        functools.partial(_route_kernel, block=block),
        grid=(1,),
        in_specs=[pl.BlockSpec((ROUTE_ROWS, n), lambda i: (0, 0)), _resident(bias.shape), _resident(tri.shape),
                  _resident(etri.shape)],
        out_specs=[pl.BlockSpec((SUBLANES, n), lambda i: (0, 0)), pl.BlockSpec((SUBLANES, n), lambda i: (0, 0)),
                   pl.BlockSpec((SUBLANES, nbp), lambda i: (0, 0)),
                   pl.BlockSpec((N_EXPERTS, LANES), lambda i: (0, 0)),
                   pl.BlockSpec((SUBLANES, n), lambda i: (0, 0))],
        out_shape=[jax.ShapeDtypeStruct((SUBLANES, n), I32), jax.ShapeDtypeStruct((SUBLANES, n), F32),
                   jax.ShapeDtypeStruct((SUBLANES, nbp), I32), jax.ShapeDtypeStruct((N_EXPERTS, LANES), I32),
                   jax.ShapeDtypeStruct((SUBLANES, n), I32)],
        scratch_shapes=[pltpu.VMEM((SUBLANES, n), F32)],
        compiler_params=pltpu.CompilerParams(dimension_semantics=("arbitrary",),
                                             vmem_limit_bytes=VMEM_LIMIT_BYTES),
        name="route",
    )(logits_t, bias, tri, etri)


def _start_token_gather(tok, src_hbm, dst, sem, row0, chunks):
    pltpu.make_async_copy(src_hbm.at[pl.ds(pl.multiple_of(tok * chunks, chunks), chunks)],
                          dst.at[pl.ds(row0, chunks)], sem).start()


def _gather_tokens(token_of, src_hbm, dst, sem, count, chunks, unrolled):
    pitch = _gather_pitch(chunks)
    if unrolled:
        for r in range(count):
            _start_token_gather(token_of(r), src_hbm, dst, sem, r * pitch, chunks)
    else:
        def body(r, _):
            _start_token_gather(token_of(r), src_hbm, dst, sem, pl.multiple_of(r * pitch, SUBLANES), chunks)
            return 0
        lax.fori_loop(0, count, body, 0, unroll=8)


def _wait_token_gathers(src_hbm, dst, sem, count, chunks):
    pltpu.make_async_copy(src_hbm.at[pl.ds(0, count * chunks)], dst.at[pl.ds(0, count * chunks)], sem).wait()


def _experts_kernel(bnd_ref, base_ref, tok_ref, x_hbm, gf_ref, wg_hbm, wu_hbm, wd_hbm, y_hbm,
                    xg, gsem, obuf, osem, zbuf, zsem, wg32, wu32, wd32, wsem, wg16, wu16, wd16, *, rows, n_blocks):
    e = pl.program_id(0)
    ne = pl.num_programs(0)
    block_rows = obuf.shape[1]
    chunks = block_rows // rows
    pitch = _gather_pitch(chunks)
    n_used = bnd_ref[ne]

    def gather(b, slot, unrolled):
        base = base_ref[b]
        _gather_tokens(lambda r: tok_ref[base + r], x_hbm, xg.at[slot], gsem.at[slot], rows, chunks, unrolled)

    def out_copy(b, slot):
        dst = y_hbm.at[pl.ds(pl.multiple_of(b * block_rows, block_rows), block_rows)]
        return pltpu.make_async_copy(obuf.at[slot], dst, osem.at[slot])

    def weight_copies(expert, slot):
        return [pltpu.make_async_copy(hbm.at[expert], buf.at[slot], wsem.at[slot])
                for hbm, buf in ((wg_hbm, wg32), (wu_hbm, wu32), (wd_hbm, wd32))]

    def zero_copy(b):
        dst = y_hbm.at[pl.ds(pl.multiple_of(b * block_rows, block_rows), block_rows)]
        return pltpu.make_async_copy(zbuf, dst, zsem.at[0])

    @pl.when(e == 0)
    def _():
        for cp in weight_copies(0, 0):
            cp.start(priority=1)
        gather(0, 0, unrolled=False)
        zbuf[...] = jnp.zeros(zbuf.shape, F32)

        def fill(b, carry):
            zero_copy(b).start(priority=1)
            return carry

        lax.fori_loop(n_used, n_blocks, fill, 0)

    @pl.when(e + 1 < ne)
    def _():
        for cp in weight_copies(e + 1, (e + 1) % 2):
            cp.start(priority=1)

    for cp in weight_copies(e, e % 2):
        cp.wait()

    @pl.when(bnd_ref[e + 1] > bnd_ref[e])
    def _():
        wg16[...] = wg32[e % 2].astype(BF16)
        wu16[...] = wu32[e % 2].astype(BF16)
        wd16[...] = wd32[e % 2].astype(BF16)

    def block_body(b, carry):
        slot = b % 2

        @pl.when(b >= 2)
        def _():
            out_copy(b - 2, slot).wait()

        _wait_token_gathers(x_hbm, xg.at[slot], gsem.at[slot], rows, chunks)
        xb = _load_token_major(xg.at[slot], rows, chunks, pitch)
        gather(jnp.minimum(b + 1, n_blocks - 1), 1 - slot, unrolled=True)
        h = _rms(xb, gf_ref[...]).astype(BF16)
        hg = jnp.dot(h, wg16[...], preferred_element_type=F32)
        hu = jnp.dot(h, wu16[...], preferred_element_type=F32)
        hid = hg * (1.0 / (1.0 + jnp.exp(-hg))) * hu
        _store_token_major(obuf.at[slot], jnp.dot(hid.astype(BF16), wd16[...], preferred_element_type=F32))
        out_copy(b, slot).start()
        return carry

    lax.fori_loop(bnd_ref[e], bnd_ref[e + 1], block_body, 0)

    @pl.when(e == ne - 1)
    def _():
        _wait_token_gathers(x_hbm, xg.at[n_used % 2], gsem.at[n_used % 2], rows, chunks)

        @pl.when(n_used >= 2)
        def _():
            out_copy(n_used - 2, n_used % 2).wait()

        out_copy(n_used - 1, (n_used - 1) % 2).wait()

        def drain(b, carry):
            zero_copy(b).wait()
            return carry

        lax.fori_loop(n_used, n_blocks, drain, 0)


def _experts(bounds, block_base, sorted_tok, x2c, g_ffn, w_gate, w_up, w_down, block):
    n_blocks = block_base.shape[0]
    ne, d, de = w_gate.shape
    chunks = d // LANES
    grid_spec = pltpu.PrefetchScalarGridSpec(
        num_scalar_prefetch=3,
        grid=(ne,),
        in_specs=[pl.BlockSpec(memory_space=pl.ANY),
                  _resident(g_ffn.shape),
                  pl.BlockSpec(memory_space=pl.ANY),
                  pl.BlockSpec(memory_space=pl.ANY),
                  pl.BlockSpec(memory_space=pl.ANY)],
        out_specs=pl.BlockSpec(memory_space=pl.ANY),
        scratch_shapes=[pltpu.VMEM((2, block * _gather_pitch(chunks), LANES), F32), pltpu.SemaphoreType.DMA((2,)),
                        pltpu.VMEM((2, block * chunks, LANES), F32), pltpu.SemaphoreType.DMA((2,)),
                        pltpu.VMEM((block * chunks, LANES), F32), pltpu.SemaphoreType.DMA((1,)),
                        pltpu.VMEM((2, d, de), F32), pltpu.VMEM((2, d, de), F32), pltpu.VMEM((2, de, d), F32),
                        pltpu.SemaphoreType.DMA((2,)),
                        pltpu.VMEM((d, de), BF16), pltpu.VMEM((d, de), BF16), pltpu.VMEM((de, d), BF16)],
    )
    return pl.pallas_call(
        functools.partial(_experts_kernel, rows=block, n_blocks=n_blocks),
        grid_spec=grid_spec,
        out_shape=jax.ShapeDtypeStruct((n_blocks * block * chunks, LANES), F32),
        compiler_params=pltpu.CompilerParams(dimension_semantics=("arbitrary",),
                                             vmem_limit_bytes=VMEM_LIMIT_BYTES),
        name="experts",
    )(bounds, block_base, sorted_tok, x2c, g_ffn, w_gate, w_up, w_down)


def _combine_kernel(d_ref, dn_ref, x_ref, w_ref, y_hbm, g_ref, o_ref, ybuf, sems):
    i = pl.program_id(0)
    nt = pl.num_programs(0)
    tm, d = o_ref.shape
    chunks = d // LANES
    slot = i % 2

    def expert_rows(k):
        return _load_token_major(ybuf.at[slot, k], tm, chunks, _gather_pitch(chunks))

    def start(idx_ref, s):
        for k in range(TOP_K):
            _gather_tokens(lambda r, k=k: idx_ref[0, 0, k * tm + r], y_hbm, ybuf.at[s, k], sems.at[s], tm, chunks,
                           unrolled=False)

    def wait(s):
        for k in range(TOP_K):
            _wait_token_gathers(y_hbm, ybuf.at[s, k], sems.at[s], tm, chunks)

    @pl.when(i == 0)
    def _():
        start(d_ref, 0)

    start(dn_ref, 1 - slot)
    wait(slot)
    w = w_ref[...]
    y = _load_token_major(x_ref, tm, chunks, chunks) + (expert_rows(0) * w[:, 0:1] + expert_rows(1) * w[:, 1:2])
    o_ref[...] = _rms(y, g_ref[...])

    @pl.when(i == nt - 1)
    def _():
        wait(1 - slot)


def _combine(dest_tiles, x2c, w_tok, ysc, g_final):
    d = g_final.shape[1]
    chunks = d // LANES
    n = x2c.shape[0] // chunks
    tm = dest_tiles.shape[2] // TOP_K
    nt = n // tm
    smem_blk = lambda f: pl.BlockSpec((1, 1, TOP_K * tm), f, memory_space=pltpu.SMEM)
    return pl.pallas_call(
        _combine_kernel,
        grid=(nt,),
        in_specs=[smem_blk(lambda i: (i, 0, 0)), smem_blk(lambda i: (jnp.minimum(i + 1, nt - 1), 0, 0)),
                  pl.BlockSpec((tm * chunks, LANES), lambda i: (i, 0)),
                  pl.BlockSpec((tm, SUBLANES), lambda i: (i, 0)), pl.BlockSpec(memory_space=pl.ANY),
                  _resident(g_final.shape)],
        out_specs=pl.BlockSpec((tm, d), lambda i: (i, 0)),
        out_shape=jax.ShapeDtypeStruct((n, d), F32),
        scratch_shapes=[pltpu.VMEM((2, TOP_K, tm * _gather_pitch(chunks), LANES), F32),
                        pltpu.SemaphoreType.DMA((2,))],
        compiler_params=pltpu.CompilerParams(dimension_semantics=("arbitrary",),
                                             vmem_limit_bytes=VMEM_LIMIT_BYTES),
        name="combine",
    )(dest_tiles, dest_tiles, x2c, w_tok, ysc, g_final)


def _layer(x, mem, norm_mix_g, w_in, w_pool, pool_scale, w_dw, b_dw, conv_ln_g, conv_ln_b, w_conv_pw,
           out_norm_a_g, out_norm_b_g, w_out, norm_xattn_g, norm_mem_g, w_q_mem, w_kv_mem, w_o_mem, norm_ffn_g,
           w_router_group, b_router_group, w_router_expert, b_router_expert, w_exp_gate, w_exp_up, w_exp_down,
           out_gain):
    n, d = x.shape
    r2 = lambda v: v.reshape(1, -1)
    cw = w_conv_pw.shape[0]
    x1 = _mixer(x, r2(norm_mix_g), w_in.astype(BF16), w_pool.astype(BF16), r2(pool_scale),
                w_dw.reshape(CONV_KERNEL, cw), r2(b_dw),
                r2(conv_ln_g), r2(conv_ln_b), w_conv_pw.astype(BF16), r2(out_norm_a_g), r2(out_norm_b_g),
                w_out.astype(BF16))

    kv = _kv(mem, r2(norm_mem_g), w_kv_mem)
    w_router_t = jnp.zeros((LOGIT_ROWS, d), F32)
    w_router_t = w_router_t.at[:N_EXPERTS].set(w_router_expert.T).at[N_EXPERTS:N_EXPERTS + N_GROUPS].set(
        w_router_group.T).astype(BF16)
    qk, vo = _fold_memory(w_q_mem, w_o_mem, kv)
    x2, logits_t = _xattn(x1, r2(norm_xattn_g), qk, vo, r2(norm_ffn_g), w_router_t)

    block = MOE_BLOCK
    nb = pl.cdiv(n * TOP_K, block) + N_EXPERTS
    bias = jnp.zeros((ROUTE_ROWS, 1), F32)
    bias = bias.at[:N_EXPERTS, 0].set(b_router_expert).at[N_EXPERTS:N_EXPERTS + N_GROUPS, 0].set(b_router_group)
    dest, w_tok, blk, bst, eg = _route(logits_t, bias, block, nb)
    dest = dest[:TOP_K]
    n_used = blk[1, 0]
    bounds = jnp.concatenate([bst[:, 0], blk[1, :1]])
    keys = jnp.sort((eg[:TOP_K] * n + jnp.arange(n, dtype=I32)).reshape(-1))
    sorted_tok = jnp.concatenate([keys % n, jnp.arange(block, dtype=I32)])
    first_row = jnp.cumsum(bst[:, 1]) - bst[:, 1]
    bidx = jnp.arange(nb, dtype=I32)
    block_e = jnp.minimum(blk[0, :nb], N_EXPERTS - 1)
    block_base = jnp.where(bidx < n_used, first_row[block_e] + (bidx - bst[block_e, 0]) * block, 0)

    ys = _experts(bounds, block_base, sorted_tok, x2, r2(norm_ffn_g), w_exp_gate, w_exp_up, w_exp_down, block)

    tm = min(COMBINE_TILE, n)
    dest_tiles = dest.reshape(TOP_K, n // tm, tm).transpose(1, 0, 2).reshape(n // tm, 1, TOP_K * tm)
    return _combine(dest_tiles, x2, w_tok.T, ys, r2(out_gain))


def kernel(x, mem, norm_mix_g, w_in, w_pool, pool_scale, w_dw, b_dw, conv_ln_g, conv_ln_b, w_conv_pw, out_norm_a_g,
           out_norm_b_g, w_out, norm_xattn_g, norm_mem_g, w_q_mem, w_kv_mem, w_o_mem, norm_ffn_g, w_router_group,
           b_router_group, w_router_expert, b_router_expert, w_exp_gate, w_exp_up, w_exp_down, final_norm_g):
    assert x.shape[0] == 1 and mem.shape[0] == 1 and norm_mix_g.shape[0] == 1
    out = _layer(x[0], mem[0], norm_mix_g[0], w_in[0], w_pool[0], pool_scale[0], w_dw[0], b_dw[0], conv_ln_g[0],
                 conv_ln_b[0], w_conv_pw[0], out_norm_a_g[0], out_norm_b_g[0], w_out[0], norm_xattn_g[0],
                 norm_mem_g[0], w_q_mem[0], w_kv_mem[0], w_o_mem[0], norm_ffn_g[0], w_router_group[0],
                 b_router_group[0], w_router_expert[0], b_router_expert[0], w_exp_gate[0], w_exp_up[0],
                 w_exp_down[0], final_norm_g)
    return out[None]
```

```python
import functools

import jax
import jax.numpy as jnp
from jax import lax
from jax.experimental import pallas as pl
from jax.experimental.pallas import tpu as pltpu

F32 = jnp.float32
BF16 = jnp.bfloat16
I32 = jnp.int32

EPS = 1e-6
POOL_WINDOWS = (2, 4, 8, 16)
CONV_KERNEL = 31
XATTN_HEADS = 4
N_GROUPS = 4
EXPERTS_PER_GROUP = 8
N_EXPERTS = N_GROUPS * EXPERTS_PER_GROUP
TOP_K = 2

SUBLANES = 8
LANES = 128
VMEM_LIMIT_BYTES = 56 * 1024 * 1024

POOL_HALO = 16
TIME_PITCH = 2
CONV_HALO = 32
MIXER_TILE = 512
XATTN_TILE = 1024
ROUTE_TILE = 512
MOE_BLOCK = 256
COMBINE_TILE = 256
ROUTE_ROWS = 40


def _rms(x, g):
    return x * lax.rsqrt(jnp.mean(x * x, axis=-1, keepdims=True) + EPS) * g


def _resident(shape):
    nd = len(shape)
    return pl.BlockSpec(shape, lambda *_: (0,) * nd, pipeline_mode=pl.Buffered(1))


def _store_token_major(ref, val):
    t, d = val.shape
    chunks = d // LANES
    for c in range(chunks):
        ref[pl.ds(c, t, stride=chunks), :] = val[:, c * LANES:(c + 1) * LANES]


def _load_token_major(ref, t, chunks, pitch):
    return jnp.concatenate([ref[pl.ds(c, t, stride=pitch), :] for c in range(chunks)], axis=-1)


def _gather_pitch(chunks):
    tiles = pl.cdiv(chunks, SUBLANES)
    return SUBLANES * (tiles + 1 - tiles % 2)


def _mixer_kernel(x_ref, gmix_ref, win_ref, wpool_ref, pscale_ref, wdw_ref, bdw_ref, lng_ref, lnb_ref,
                  wpw_ref, ga_ref, gb_ref, wout_ref, o_ref, hbuf, ubuf, ybuf, cbuf, cbf, ya_new, ycat):
    j = pl.program_id(0)
    tm = x_ref.shape[0]
    pw = ubuf.shape[0] * LANES
    cw = ybuf.shape[0] * LANES
    gdim = pw // len(POOL_WINDOWS)
    col = gdim

    @pl.when(j == 0)
    def _():
        ubuf[:, 0:TIME_PITCH * POOL_HALO, :] = jnp.zeros((pw // LANES, TIME_PITCH * POOL_HALO, LANES), F32)
        ybuf[:, 0:TIME_PITCH * CONV_HALO, :] = jnp.zeros((cw // LANES, TIME_PITCH * CONV_HALO, LANES), F32)

    def frames(buf, slab, r, count):
        return buf[slab, pl.ds(TIME_PITCH * r, count, stride=TIME_PITCH), :]

    def put_frames(buf, slab0, r, val):
        for s in range(val.shape[1] // LANES):
            buf[slab0 + s, pl.ds(TIME_PITCH * r, val.shape[0], stride=TIME_PITCH), :] = val[:, s * LANES:(s + 1) * LANES]

    def keep_halo(buf, halo):
        buf[:, 0:TIME_PITCH * halo, :] = buf[:, TIME_PITCH * tm:TIME_PITCH * (tm + halo), :]

    def norm_in():
        hbuf[...] = _rms(x_ref[...], gmix_ref[...]).astype(BF16)

    def proj_pool(c0):
        def run():
            u = jnp.dot(hbuf[...], win_ref[:, c0:c0 + col], preferred_element_type=F32)
            put_frames(ubuf, c0 // LANES, POOL_HALO, u)
        return run

    def proj_glu(c0):
        def run():
            a = jnp.dot(hbuf[...], win_ref[:, pw + c0:pw + c0 + col], preferred_element_type=F32)
            g = jnp.dot(hbuf[...], win_ref[:, pw + cw + c0:pw + cw + c0 + col], preferred_element_type=F32)
            put_frames(ybuf, c0 // LANES, CONV_HALO, a * (1.0 / (1.0 + jnp.exp(-g))))
        return run

    def pool_group(g, w):
        def run():
            pos = j * tm + lax.broadcasted_iota(I32, (tm, 1), 0)
            cols, toks = [], []
            for s in range(g * gdim // LANES, (g + 1) * gdim // LANES):
                tok = frames(ubuf, s, POOL_HALO, tm)
                wsum = tok
                for k in range(1, w):
                    wsum = wsum + frames(ubuf, s, POOL_HALO - k, tm)
                cols.append(wsum)
                toks.append(tok)
            cnt = jnp.minimum(pos + 1, w).astype(F32)
            d = jnp.concatenate(cols, axis=-1) / cnt - jnp.concatenate(toks, axis=-1)
            z = jnp.dot(d.astype(BF16), wpool_ref[g], preferred_element_type=F32)
            ya_new[:, g * gdim:(g + 1) * gdim] = z * pscale_ref[:, g * gdim:(g + 1) * gdim]
        return run

    def norm_pool():
        ya_new[...] = _rms(ya_new[...], ga_ref[...])

    first = CONV_HALO - (CONV_KERNEL - 1)
    row_chunk = 64

    def conv_chunk(r0, s):
        def run():
            acc = None
            for k in range(CONV_KERNEL):
                term = frames(ybuf, s, first + r0 + k, row_chunk) * wdw_ref[k:k + 1, s * LANES:(s + 1) * LANES]
                acc = term if acc is None else acc + term
            cbuf[r0:r0 + row_chunk, s * LANES:(s + 1) * LANES] = acc
        return run

    def norm_conv():
        c = cbuf[...] + bdw_ref[...]
        mu = jnp.mean(c, axis=-1, keepdims=True)
        var = jnp.mean(jnp.square(c - mu), axis=-1, keepdims=True)
        c = (c - mu) * lax.rsqrt(var + EPS) * lng_ref[...] + lnb_ref[...]
        cbf[...] = (c * (1.0 / (1.0 + jnp.exp(-c)))).astype(BF16)

    def pointwise():
        yb = jnp.dot(cbf[...], wpw_ref[...], preferred_element_type=F32)
        ycat[:, 0:pw] = ya_new[...].astype(BF16)
        ycat[:, pw:pw + cw] = _rms(yb, gb_ref[...]).astype(BF16)

    def out_cols(c0):
        def run():
            o_ref[:, c0:c0 + col] = x_ref[:, c0:c0 + col] + jnp.dot(
                ycat[...], wout_ref[:, c0:c0 + col], preferred_element_type=F32)
        return run

    norm_in()
    for c0 in range(0, cw, col):
        if c0 < pw:
            proj_pool(c0)()
        proj_glu(c0)()
        if c0 < pw:
            pool_group(c0 // gdim, POOL_WINDOWS[c0 // gdim])()
        for s in range(c0 // LANES, (c0 + col) // LANES):
            for r0 in range(0, tm, row_chunk):
                conv_chunk(r0, s)()
    norm_pool()
    norm_conv()
    pointwise()
    for c0 in range(0, x_ref.shape[1], col):
        out_cols(c0)()
    keep_halo(ubuf, POOL_HALO)
    keep_halo(ybuf, CONV_HALO)


def _mixer(x, g_mix, w_in, w_pool, pool_scale, w_dw, b_dw, ln_g, ln_b, w_pw, g_a, g_b, w_out):
    n, d = x.shape
    pw = w_pool.shape[0] * w_pool.shape[1]
    cw = w_pw.shape[0]
    tm = min(MIXER_TILE, n)
    assert cw % (pw // len(POOL_WINDOWS)) == 0 and pw <= cw
    row = pl.BlockSpec((tm, d), lambda j: (j, 0))
    return pl.pallas_call(
        _mixer_kernel,
        grid=(n // tm,),
        in_specs=[row, _resident(g_mix.shape), _resident(w_in.shape), _resident(w_pool.shape),
                  _resident(pool_scale.shape), _resident(w_dw.shape), _resident(b_dw.shape),
                  _resident(ln_g.shape), _resident(ln_b.shape), _resident(w_pw.shape),
                  _resident(g_a.shape), _resident(g_b.shape), _resident(w_out.shape)],
        out_specs=row,
        out_shape=jax.ShapeDtypeStruct((n, d), F32),
        scratch_shapes=[pltpu.VMEM((tm, d), BF16),
                        pltpu.VMEM((pw // LANES, TIME_PITCH * (tm + POOL_HALO), LANES), F32),
                        pltpu.VMEM((cw // LANES, TIME_PITCH * (tm + CONV_HALO), LANES), F32),
                        pltpu.VMEM((tm, cw), F32),
                        pltpu.VMEM((tm, cw), BF16),
                        pltpu.VMEM((tm, pw), F32),
                        pltpu.VMEM((tm, pw + cw), BF16)],
        compiler_params=pltpu.CompilerParams(dimension_semantics=("arbitrary",),
                                             vmem_limit_bytes=VMEM_LIMIT_BYTES),
        name="mixer",
    )(x, g_mix, w_in, w_pool, pool_scale, w_dw, b_dw, ln_g, ln_b, w_pw, g_a, g_b, w_out)


def _memory_kernel(mem_ref, g_ref, wkv_ref, wq_ref, wo_ref, a_ref, b_ref):
    j = pl.program_id(0)
    m = _rms(mem_ref[...], g_ref[...]).astype(BF16)
    kv = jnp.dot(m, wkv_ref[...].astype(BF16), preferred_element_type=F32).astype(BF16)

    @pl.when(j < XATTN_HEADS)
    def _():
        a_ref[...] = lax.dot_general(wq_ref[...].astype(BF16), kv, (((1,), (1,)), ((), ())),
                                     preferred_element_type=F32).astype(BF16)

    @pl.when(j >= XATTN_HEADS)
    def _():
        b_ref[...] = jnp.dot(kv, wo_ref[...].astype(BF16), preferred_element_type=F32).astype(BF16)


def _memory(mem, g_mem, w_kv, w_q, w_o):
    m, d = mem.shape
    nh = XATTN_HEADS
    hd = d // nh
    return pl.pallas_call(
        _memory_kernel,
        grid=(2 * nh,),
        in_specs=[_resident((m, d)), _resident(g_mem.shape),
                  pl.BlockSpec((d, hd), lambda j: (0, j)),
                  pl.BlockSpec((d, hd), lambda j: (0, jnp.minimum(j, nh - 1))),
                  pl.BlockSpec((hd, d), lambda j: (jnp.maximum(j - nh, 0), 0))],
        out_specs=[pl.BlockSpec((d, m), lambda j: (0, jnp.minimum(j, nh - 1))),
                   pl.BlockSpec((m, d), lambda j: (jnp.maximum(j - nh, 0), 0))],
        out_shape=[jax.ShapeDtypeStruct((d, nh * m), BF16), jax.ShapeDtypeStruct((nh * m, d), BF16)],
        compiler_params=pltpu.CompilerParams(dimension_semantics=("arbitrary",),
                                             vmem_limit_bytes=VMEM_LIMIT_BYTES),
        name="memory",
    )(mem, g_mem, w_kv, w_q, w_o)


def _xattn_kernel(x_ref, gx_ref, a_ref, b_ref, gf_ref, wr_ref, o_ref, lg_ref):
    d = x_ref.shape[1]
    hd = d // XATTN_HEADS
    m = a_ref.shape[1] // XATTN_HEADS
    x = x_ref[...]
    h = _rms(x, gx_ref[...])
    s_all = jnp.dot(h.astype(BF16), a_ref[...], preferred_element_type=F32)
    ps = []
    for hh in range(XATTN_HEADS):
        s = s_all[:, hh * m:(hh + 1) * m] * (hd ** -0.5)
        e = jnp.exp(s - jnp.max(s, axis=-1, keepdims=True))
        ps.append((e / jnp.sum(e, axis=-1, keepdims=True)).astype(BF16))
    x2 = x + jnp.dot(jnp.concatenate(ps, axis=-1), b_ref[...], preferred_element_type=F32)
    _store_token_major(o_ref, x2)
    h3 = _rms(x2, gf_ref[...])
    lg_ref[...] = lax.dot_general(wr_ref[...], h3.astype(BF16), (((1,), (1,)), ((), ())),
                                  preferred_element_type=F32)


def _xattn(x, g_x, qk, vo, g_ffn, w_router_t):
    n, d = x.shape
    tm = min(XATTN_TILE, n)
    chunks = d // LANES
    row = pl.BlockSpec((tm, d), lambda i: (i, 0))
    return pl.pallas_call(
        _xattn_kernel,
        grid=(n // tm,),
        in_specs=[row, _resident(g_x.shape), _resident(qk.shape), _resident(vo.shape),
                  _resident(g_ffn.shape), _resident(w_router_t.shape)],
        out_specs=[pl.BlockSpec((tm * chunks, LANES), lambda i: (i, 0)),
                   pl.BlockSpec((ROUTE_ROWS, tm), lambda i: (0, i))],
        out_shape=[jax.ShapeDtypeStruct((n * chunks, LANES), F32), jax.ShapeDtypeStruct((ROUTE_ROWS, n), F32)],
        compiler_params=pltpu.CompilerParams(dimension_semantics=("arbitrary",),
                                             vmem_limit_bytes=VMEM_LIMIT_BYTES),
        name="xattn",
    )(x, g_x, qk, vo, g_ffn, w_router_t)


def _first_argmax(v, rows):
    m = jnp.max(v, axis=0, keepdims=True)
    idx = jnp.min(jnp.where(v == m, rows.astype(F32), float(v.shape[0])), axis=0, keepdims=True)
    return m, idx.astype(I32)


def _route_kernel(lg_ref, bias_ref, tri_ref, etri_ref, dest_ref, wtok_ref, blk_ref, bst_ref, eg_buf, rank_buf,
                  *, block):
    n = lg_ref.shape[1]
    t = tri_ref.shape[0]
    n_tiles = n // t
    epg = EXPERTS_PER_GROUP
    rows_g = lax.broadcasted_iota(I32, (N_GROUPS, t), 0)
    rows_e = lax.broadcasted_iota(I32, (epg, t), 0)
    rows_all = lax.broadcasted_iota(I32, (N_EXPERTS, t), 0)

    def tile_body(j, carry):
        sl = pl.ds(pl.multiple_of(j * t, t), t)
        le = lg_ref[0:N_EXPERTS, sl] + bias_ref[0:N_EXPERTS, :]
        lgp = lg_ref[N_EXPERTS:N_EXPERTS + N_GROUPS, sl] + bias_ref[N_EXPERTS:N_EXPERTS + N_GROUPS, :]
        eg_ = jnp.exp(lgp - jnp.max(lgp, axis=0, keepdims=True))
        pg = eg_ / jnp.sum(eg_, axis=0, keepdims=True)
        pg_top, g_idx = _first_argmax(pg, rows_g)
        le_sel = jnp.zeros((epg, t), F32)
        for g in range(N_GROUPS):
            le_sel = jnp.where(g_idx == g, le[g * epg:(g + 1) * epg, :], le_sel)
        ee = jnp.exp(le_sel - jnp.max(le_sel, axis=0, keepdims=True))
        pe = ee / jnp.sum(ee, axis=0, keepdims=True)
        p1, i1 = _first_argmax(pe, rows_e)
        p2, i2 = _first_argmax(jnp.where(rows_e == i1, -1.0, pe), rows_e)
        psum = p1 + p2
        w1 = pg_top * p1 / psum
        w2 = pg_top * p2 / psum
        e1 = g_idx * epg + i1
        e2 = g_idx * epg + i2
        oh1 = rows_all == e1
        oh2 = rows_all == e2
        onehot = jnp.where(oh1 | oh2, 1.0, 0.0)
        prefix = carry + jnp.dot(onehot.astype(BF16), tri_ref[...], preferred_element_type=F32)
        r1 = jnp.sum(jnp.where(oh1, prefix, 0.0), axis=0, keepdims=True)
        r2 = jnp.sum(jnp.where(oh2, prefix, 0.0), axis=0, keepdims=True)
        eg_buf[0:1, sl] = e1
        eg_buf[1:2, sl] = e2
        rank_buf[0:1, sl] = r1
        rank_buf[1:2, sl] = r2
        wtok_ref[0:1, sl] = w1
        wtok_ref[1:2, sl] = w2
        return carry + jnp.sum(onehot, axis=1, keepdims=True)

    counts = lax.fori_loop(0, n_tiles, tile_body, jnp.zeros((N_EXPERTS, 1), F32))
    wtok_ref[2:SUBLANES, :] = jnp.zeros((SUBLANES - 2, n), F32)

    nblk = jnp.floor((counts + (block - 1)) * (1.0 / block))
    nblk_b = jnp.broadcast_to(nblk, (N_EXPERTS, LANES)).astype(BF16)
    bstart = jnp.dot(etri_ref[...], nblk_b, preferred_element_type=F32)[:, 0:1]
    bend = bstart + nblk
    pstart = bstart * block

    def dest_body(j, _):
        sl = pl.ds(pl.multiple_of(j * t, t), t)
        for k in range(TOP_K):
            oh = rows_all == eg_buf[k:k + 1, sl]
            base = jnp.sum(jnp.where(oh, pstart, 0.0), axis=0, keepdims=True)
            dest_ref[k:k + 1, sl] = (base + rank_buf[k:k + 1, sl]).astype(I32)
        return 0

    lax.fori_loop(0, n_tiles, dest_body, 0)
    dest_ref[2:SUBLANES, :] = jnp.zeros((SUBLANES - 2, n), I32)

    nbp = blk_ref.shape[1]
    bidx = lax.broadcasted_iota(I32, (N_EXPERTS, nbp), 1).astype(F32)
    be = jnp.sum(jnp.where(bend <= bidx, 1.0, 0.0), axis=0, keepdims=True)
    n_used = jnp.max(bend, axis=0, keepdims=True)
    blk_ref[0:1, :] = be.astype(I32)
    blk_ref[1:2, :] = jnp.broadcast_to(n_used, (1, nbp)).astype(I32)
    blk_ref[2:SUBLANES, :] = jnp.zeros((SUBLANES - 2, nbp), I32)
    lane = lax.broadcasted_iota(I32, bst_ref.shape, 1)
    bst_ref[...] = jnp.where(lane == 1, counts, bstart).astype(I32)
    eg_buf[2:SUBLANES, :] = jnp.zeros((SUBLANES - 2, n), I32)


def _route(logits_t, bias, block, nb):
    n = logits_t.shape[1]
    t = min(ROUTE_TILE, n)
    nbp = pl.cdiv(nb, LANES) * LANES
    assert nbp <= 256, "block counts must stay exactly representable in bf16"
    tri = (lax.broadcasted_iota(I32, (t, t), 0) < lax.broadcasted_iota(I32, (t, t), 1)).astype(BF16)
    etri = (lax.broadcasted_iota(I32, (N_EXPERTS, N_EXPERTS), 1)
            < lax.broadcasted_iota(I32, (N_EXPERTS, N_EXPERTS), 0)).astype(BF16)
    return pl.pallas_call(
        functools.partial(_route_kernel, block=block),
        grid=(1,),
        in_specs=[pl.BlockSpec((ROUTE_ROWS, n), lambda i: (0, 0)), _resident(bias.shape), _resident(tri.shape),
                  _resident(etri.shape)],
        out_specs=[pl.BlockSpec((SUBLANES, n), lambda i: (0, 0)), pl.BlockSpec((SUBLANES, n), lambda i: (0, 0)),
                   pl.BlockSpec((SUBLANES, nbp), lambda i: (0, 0)),
                   pl.BlockSpec((N_EXPERTS, LANES), lambda i: (0, 0)),
                   pl.BlockSpec((SUBLANES, n), lambda i: (0, 0))],
        out_shape=[jax.ShapeDtypeStruct((SUBLANES, n), I32), jax.ShapeDtypeStruct((SUBLANES, n), F32),
                   jax.ShapeDtypeStruct((SUBLANES, nbp), I32), jax.ShapeDtypeStruct((N_EXPERTS, LANES), I32),
                   jax.ShapeDtypeStruct((SUBLANES, n), I32)],
        scratch_shapes=[pltpu.VMEM((SUBLANES, n), F32)],
        compiler_params=pltpu.CompilerParams(dimension_semantics=("arbitrary",),
                                             vmem_limit_bytes=VMEM_LIMIT_BYTES),
        name="route",
    )(logits_t, bias, tri, etri)


def _start_token_gather(tok, src_hbm, dst, sem, row0, chunks):
    pltpu.make_async_copy(src_hbm.at[pl.ds(pl.multiple_of(tok * chunks, chunks), chunks)],
                          dst.at[pl.ds(row0, chunks)], sem).start()


def _gather_tokens(token_of, src_hbm, dst, sem, count, chunks, unrolled):
    pitch = _gather_pitch(chunks)
    if unrolled:
        for r in range(count):
            _start_token_gather(token_of(r), src_hbm, dst, sem, r * pitch, chunks)
    else:
        def body(r, _):
            _start_token_gather(token_of(r), src_hbm, dst, sem, pl.multiple_of(r * pitch, SUBLANES), chunks)
            return 0
        lax.fori_loop(0, count, body, 0, unroll=8)


def _wait_token_gathers(src_hbm, dst, sem, count, chunks):
    pltpu.make_async_copy(src_hbm.at[pl.ds(0, count * chunks)], dst.at[pl.ds(0, count * chunks)], sem).wait()


def _experts_kernel(bnd_ref, base_ref, tok_ref, x_hbm, gf_ref, wg_hbm, wu_hbm, wd_hbm, y_hbm,
                    xg, gsem, obuf, osem, zbuf, zsem, wg32, wu32, wd32, wsem, wg16, wu16, wd16, *, rows, n_blocks):
    e = pl.program_id(0)
    ne = pl.num_programs(0)
    block_rows = obuf.shape[1]
    chunks = block_rows // rows
    pitch = _gather_pitch(chunks)
    n_used = bnd_ref[ne]

    def gather(b, slot, unrolled):
        base = base_ref[b]
        _gather_tokens(lambda r: tok_ref[base + r], x_hbm, xg.at[slot], gsem.at[slot], rows, chunks, unrolled)

    def out_copy(b, slot):
        dst = y_hbm.at[pl.ds(pl.multiple_of(b * block_rows, block_rows), block_rows)]
        return pltpu.make_async_copy(obuf.at[slot], dst, osem.at[slot])

    def weight_copies(expert, slot):
        return [pltpu.make_async_copy(hbm.at[expert], buf.at[slot], wsem.at[slot])
                for hbm, buf in ((wg_hbm, wg32), (wu_hbm, wu32), (wd_hbm, wd32))]

    def zero_copy(b):
        dst = y_hbm.at[pl.ds(pl.multiple_of(b * block_rows, block_rows), block_rows)]
        return pltpu.make_async_copy(zbuf, dst, zsem.at[0])

    @pl.when(e == 0)
    def _():
        for cp in weight_copies(0, 0):
            cp.start(priority=1)
        gather(0, 0, unrolled=False)
        zbuf[...] = jnp.zeros(zbuf.shape, F32)

        def fill(b, carry):
            zero_copy(b).start(priority=1)
            return carry

        lax.fori_loop(n_used, n_blocks, fill, 0)

    @pl.when(e + 1 < ne)
    def _():
        for cp in weight_copies(e + 1, (e + 1) % 2):
            cp.start(priority=1)

    for cp in weight_copies(e, e % 2):
        cp.wait()

    @pl.when(bnd_ref[e + 1] > bnd_ref[e])
    def _():
        wg16[...] = wg32[e % 2].astype(BF16)
        wu16[...] = wu32[e % 2].astype(BF16)
        wd16[...] = wd32[e % 2].astype(BF16)

    def block_body(b, carry):
        slot = b % 2

        @pl.when(b >= 2)
        def _():
            out_copy(b - 2, slot).wait()

        _wait_token_gathers(x_hbm, xg.at[slot], gsem.at[slot], rows, chunks)
        xb = _load_token_major(xg.at[slot], rows, chunks, pitch)
        gather(jnp.minimum(b + 1, n_blocks - 1), 1 - slot, unrolled=True)
        h = _rms(xb, gf_ref[...]).astype(BF16)
        hg = jnp.dot(h, wg16[...], preferred_element_type=F32)
        hu = jnp.dot(h, wu16[...], preferred_element_type=F32)
        hid = hg * (1.0 / (1.0 + jnp.exp(-hg))) * hu
        _store_token_major(obuf.at[slot], jnp.dot(hid.astype(BF16), wd16[...], preferred_element_type=F32))
        out_copy(b, slot).start()
        return carry

    lax.fori_loop(bnd_ref[e], bnd_ref[e + 1], block_body, 0)

    @pl.when(e == ne - 1)
    def _():
        _wait_token_gathers(x_hbm, xg.at[n_used % 2], gsem.at[n_used % 2], rows, chunks)

        @pl.when(n_used >= 2)
        def _():
            out_copy(n_used - 2, n_used % 2).wait()

        out_copy(n_used - 1, (n_used - 1) % 2).wait()

        def drain(b, carry):
            zero_copy(b).wait()
            return carry

        lax.fori_loop(n_used, n_blocks, drain, 0)


def _experts(bounds, block_base, sorted_tok, x2c, g_ffn, w_gate, w_up, w_down, block):
    n_blocks = block_base.shape[0]
    ne, d, de = w_gate.shape
    chunks = d // LANES
    grid_spec = pltpu.PrefetchScalarGridSpec(
        num_scalar_prefetch=3,
        grid=(ne,),
        in_specs=[pl.BlockSpec(memory_space=pl.ANY),
                  _resident(g_ffn.shape),
                  pl.BlockSpec(memory_space=pl.ANY),
                  pl.BlockSpec(memory_space=pl.ANY),
                  pl.BlockSpec(memory_space=pl.ANY)],
        out_specs=pl.BlockSpec(memory_space=pl.ANY),
        scratch_shapes=[pltpu.VMEM((2, block * _gather_pitch(chunks), LANES), F32), pltpu.SemaphoreType.DMA((2,)),
                        pltpu.VMEM((2, block * chunks, LANES), F32), pltpu.SemaphoreType.DMA((2,)),
                        pltpu.VMEM((block * chunks, LANES), F32), pltpu.SemaphoreType.DMA((1,)),
                        pltpu.VMEM((2, d, de), F32), pltpu.VMEM((2, d, de), F32), pltpu.VMEM((2, de, d), F32),
                        pltpu.SemaphoreType.DMA((2,)),
                        pltpu.VMEM((d, de), BF16), pltpu.VMEM((d, de), BF16), pltpu.VMEM((de, d), BF16)],
    )
    return pl.pallas_call(
        functools.partial(_experts_kernel, rows=block, n_blocks=n_blocks),
        grid_spec=grid_spec,
        out_shape=jax.ShapeDtypeStruct((n_blocks * block * chunks, LANES), F32),
        compiler_params=pltpu.CompilerParams(dimension_semantics=("arbitrary",),
                                             vmem_limit_bytes=VMEM_LIMIT_BYTES),
        name="experts",
    )(bounds, block_base, sorted_tok, x2c, g_ffn, w_gate, w_up, w_down)


def _combine_kernel(d_ref, dn_ref, x_ref, w_ref, y_hbm, g_ref, o_ref, ybuf, sems):
    i = pl.program_id(0)
    nt = pl.num_programs(0)
    tm, d = o_ref.shape
    chunks = d // LANES
    slot = i % 2

    def expert_rows(k):
        return _load_token_major(ybuf.at[slot, k], tm, chunks, _gather_pitch(chunks))

    def start(idx_ref, s):
        for k in range(TOP_K):
            _gather_tokens(lambda r, k=k: idx_ref[0, 0, k * tm + r], y_hbm, ybuf.at[s, k], sems.at[s], tm, chunks,
                           unrolled=False)

    def wait(s):
        for k in range(TOP_K):
            _wait_token_gathers(y_hbm, ybuf.at[s, k], sems.at[s], tm, chunks)

    @pl.when(i == 0)
    def _():
        start(d_ref, 0)

    start(dn_ref, 1 - slot)
    wait(slot)
    w = w_ref[...]
    y = _load_token_major(x_ref, tm, chunks, chunks) + (expert_rows(0) * w[:, 0:1] + expert_rows(1) * w[:, 1:2])
    o_ref[...] = _rms(y, g_ref[...])

    @pl.when(i == nt - 1)
    def _():
        wait(1 - slot)


def _combine(dest_tiles, x2c, w_tok, ysc, g_final):
    d = g_final.shape[1]
    chunks = d // LANES
    n = x2c.shape[0] // chunks
    tm = dest_tiles.shape[2] // TOP_K
    nt = n // tm
    smem_blk = lambda f: pl.BlockSpec((1, 1, TOP_K * tm), f, memory_space=pltpu.SMEM)
    return pl.pallas_call(
        _combine_kernel,
        grid=(nt,),
        in_specs=[smem_blk(lambda i: (i, 0, 0)), smem_blk(lambda i: (jnp.minimum(i + 1, nt - 1), 0, 0)),
                  pl.BlockSpec((tm * chunks, LANES), lambda i: (i, 0)),
                  pl.BlockSpec((tm, SUBLANES), lambda i: (i, 0)), pl.BlockSpec(memory_space=pl.ANY),
                  _resident(g_final.shape)],
        out_specs=pl.BlockSpec((tm, d), lambda i: (i, 0)),
        out_shape=jax.ShapeDtypeStruct((n, d), F32),
        scratch_shapes=[pltpu.VMEM((2, TOP_K, tm * _gather_pitch(chunks), LANES), F32),
                        pltpu.SemaphoreType.DMA((2,))],
        compiler_params=pltpu.CompilerParams(dimension_semantics=("arbitrary",),
                                             vmem_limit_bytes=VMEM_LIMIT_BYTES),
        name="combine",
    )(dest_tiles, dest_tiles, x2c, w_tok, ysc, g_final)


def _layer(x, mem, norm_mix_g, w_in, w_pool, pool_scale, w_dw, b_dw, conv_ln_g, conv_ln_b, w_conv_pw,
           out_norm_a_g, out_norm_b_g, w_out, norm_xattn_g, norm_mem_g, w_q_mem, w_kv_mem, w_o_mem, norm_ffn_g,
           w_router_group, b_router_group, w_router_expert, b_router_expert, w_exp_gate, w_exp_up, w_exp_down,
           out_gain):
    n, d = x.shape
    r2 = lambda v: v.reshape(1, -1)
    cw = w_conv_pw.shape[0]
    x1 = _mixer(x, r2(norm_mix_g), w_in.astype(BF16), w_pool.astype(BF16), r2(pool_scale),
                w_dw.reshape(CONV_KERNEL, cw), r2(b_dw),
                r2(conv_ln_g), r2(conv_ln_b), w_conv_pw.astype(BF16), r2(out_norm_a_g), r2(out_norm_b_g),
                w_out.astype(BF16))

    qk, vo = _memory(mem, r2(norm_mem_g), w_kv_mem, w_q_mem, w_o_mem)
    pad_rows = ROUTE_ROWS - N_EXPERTS - N_GROUPS
    w_router_t = jnp.concatenate([w_router_expert.T, w_router_group.T, jnp.zeros((pad_rows, d), F32)]).astype(BF16)
    x2, logits_t = _xattn(x1, r2(norm_xattn_g), qk, vo, r2(norm_ffn_g), w_router_t)

    block = MOE_BLOCK
    nb = pl.cdiv(n * TOP_K, block) + N_EXPERTS
    bias = jnp.concatenate([b_router_expert, b_router_group, jnp.zeros((pad_rows,), F32)]).reshape(ROUTE_ROWS, 1)
    dest, w_tok, blk, bst, eg = _route(logits_t, bias, block, nb)
    dest = dest[:TOP_K]
    n_used = blk[1, 0]
    bounds = jnp.concatenate([bst[:, 0], blk[1, :1]])
    keys = jnp.sort((eg[:TOP_K] * n + jnp.arange(n, dtype=I32)).reshape(-1))
    sorted_tok = jnp.concatenate([keys % n, jnp.arange(block, dtype=I32)])
    first_row = jnp.cumsum(bst[:, 1]) - bst[:, 1]
    bidx = jnp.arange(nb, dtype=I32)
    block_e = jnp.minimum(blk[0, :nb], N_EXPERTS - 1)
    block_base = jnp.where(bidx < n_used, first_row[block_e] + (bidx - bst[block_e, 0]) * block, 0)

    ys = _experts(bounds, block_base, sorted_tok, x2, r2(norm_ffn_g), w_exp_gate, w_exp_up, w_exp_down, block)

    tm = min(COMBINE_TILE, n)
    dest_tiles = dest.reshape(TOP_K, n // tm, tm).transpose(1, 0, 2).reshape(n // tm, 1, TOP_K * tm)
    return _combine(dest_tiles, x2, w_tok.T, ys, r2(out_gain))


def kernel(x, mem, norm_mix_g, w_in, w_pool, pool_scale, w_dw, b_dw, conv_ln_g, conv_ln_b, w_conv_pw, out_norm_a_g,
           out_norm_b_g, w_out, norm_xattn_g, norm_mem_g, w_q_mem, w_kv_mem, w_o_mem, norm_ffn_g, w_router_group,
           b_router_group, w_router_expert, b_router_expert, w_exp_gate, w_exp_up, w_exp_down, final_norm_g):
    assert x.shape[0] == 1 and mem.shape[0] == 1 and norm_mix_g.shape[0] == 1
    out = _layer(x[0], mem[0], norm_mix_g[0], w_in[0], w_pool[0], pool_scale[0], w_dw[0], b_dw[0], conv_ln_g[0],
                 conv_ln_b[0], w_conv_pw[0], out_norm_a_g[0], out_norm_b_g[0], w_out[0], norm_xattn_g[0],
                 norm_mem_g[0], w_q_mem[0], w_kv_mem[0], w_o_mem[0], norm_ffn_g[0], w_router_group[0],
                 b_router_group[0], w_router_expert[0], b_router_expert[0], w_exp_gate[0], w_exp_up[0],
                 w_exp_down[0], final_norm_g)
    return out[None]
```

```python
import functools

import jax
import jax.numpy as jnp
from jax import lax
from jax.experimental import pallas as pl
from jax.experimental.pallas import tpu as pltpu

F32 = jnp.float32
BF16 = jnp.bfloat16
I32 = jnp.int32

EPS = 1e-6
POOL_WINDOWS = (2, 4, 8, 16)
CONV_KERNEL = 31
XATTN_HEADS = 4
N_GROUPS = 4
EXPERTS_PER_GROUP = 8
N_EXPERTS = N_GROUPS * EXPERTS_PER_GROUP
TOP_K = 2

SUBLANES = 8
LANES = 128
VMEM_LIMIT_BYTES = 56 * 1024 * 1024

POOL_HALO = 16
TIME_PITCH = 2
CONV_HALO = 32
MIXER_TILE = 512
XATTN_TILE = 1024
ROUTE_TILE = 512
MOE_BLOCK = 256
COMBINE_TILE = 256
COMBINE_DEPTH = 3
ROUTE_ROWS = 40


def _rms(x, g):
    return x * lax.rsqrt(jnp.mean(x * x, axis=-1, keepdims=True) + EPS) * g


def _resident(shape):
    nd = len(shape)
    return pl.BlockSpec(shape, lambda *_: (0,) * nd, pipeline_mode=pl.Buffered(1))


def _store_token_major(ref, val):
    t, d = val.shape
    chunks = d // LANES
    for c in range(chunks):
        ref[pl.ds(c, t, stride=chunks), :] = val[:, c * LANES:(c + 1) * LANES]


def _load_token_major(ref, t, chunks, pitch):
    return jnp.concatenate([ref[pl.ds(c, t, stride=pitch), :] for c in range(chunks)], axis=-1)


def _gather_pitch(chunks):
    tiles = pl.cdiv(chunks, SUBLANES)
    return SUBLANES * (tiles + 1 - tiles % 2)


def _mixer_kernel(x_ref, gmix_ref, win_ref, wpool_ref, pscale_ref, wdw_ref, bdw_ref, lng_ref, lnb_ref,
                  wpw_ref, ga_ref, gb_ref, wout_ref, o_ref, hbuf, ubuf, ybuf, cbuf, cbf, ya_new, ycat):
    j = pl.program_id(0)
    tm = x_ref.shape[0]
    pw = ubuf.shape[0] * LANES
    cw = ybuf.shape[0] * LANES
    gdim = pw // len(POOL_WINDOWS)
    col = gdim

    @pl.when(j == 0)
    def _():
        ubuf[:, 0:TIME_PITCH * POOL_HALO, :] = jnp.zeros((pw // LANES, TIME_PITCH * POOL_HALO, LANES), F32)
        ybuf[:, 0:TIME_PITCH * CONV_HALO, :] = jnp.zeros((cw // LANES, TIME_PITCH * CONV_HALO, LANES), F32)

    def frames(buf, slab, r, count):
        return buf[slab, pl.ds(TIME_PITCH * r, count, stride=TIME_PITCH), :]

    def put_frames(buf, slab0, r, val):
        for s in range(val.shape[1] // LANES):
            buf[slab0 + s, pl.ds(TIME_PITCH * r, val.shape[0], stride=TIME_PITCH), :] = val[:, s * LANES:(s + 1) * LANES]

    def keep_halo(buf, halo):
        buf[:, 0:TIME_PITCH * halo, :] = buf[:, TIME_PITCH * tm:TIME_PITCH * (tm + halo), :]

    def norm_in():
        hbuf[...] = _rms(x_ref[...], gmix_ref[...]).astype(BF16)

    def proj_pool(c0):
        def run():
            u = jnp.dot(hbuf[...], win_ref[:, c0:c0 + col], preferred_element_type=F32)
            put_frames(ubuf, c0 // LANES, POOL_HALO, u)
        return run

    def proj_glu(c0):
        def run():
            a = jnp.dot(hbuf[...], win_ref[:, pw + c0:pw + c0 + col], preferred_element_type=F32)
            g = jnp.dot(hbuf[...], win_ref[:, pw + cw + c0:pw + cw + c0 + col], preferred_element_type=F32)
            put_frames(ybuf, c0 // LANES, CONV_HALO, a * (1.0 / (1.0 + jnp.exp(-g))))
        return run

    def pool_group(g, w):
        def run():
            pos = j * tm + lax.broadcasted_iota(I32, (tm, 1), 0)
            cols, toks = [], []
            for s in range(g * gdim // LANES, (g + 1) * gdim // LANES):
                tok = frames(ubuf, s, POOL_HALO, tm)
                wsum = tok
                for k in range(1, w):
                    wsum = wsum + frames(ubuf, s, POOL_HALO - k, tm)
                cols.append(wsum)
                toks.append(tok)
            cnt = jnp.minimum(pos + 1, w).astype(F32)
            d = jnp.concatenate(cols, axis=-1) / cnt - jnp.concatenate(toks, axis=-1)
            z = jnp.dot(d.astype(BF16), wpool_ref[g], preferred_element_type=F32)
            ya_new[:, g * gdim:(g + 1) * gdim] = z * pscale_ref[:, g * gdim:(g + 1) * gdim]
        return run

    def norm_pool():
        ya_new[...] = _rms(ya_new[...], ga_ref[...])

    first = CONV_HALO - (CONV_KERNEL - 1)
    row_chunk = 64

    def conv_chunk(r0, s):
        def run():
            acc = None
            for k in range(CONV_KERNEL):
                term = frames(ybuf, s, first + r0 + k, row_chunk) * wdw_ref[k:k + 1, s * LANES:(s + 1) * LANES]
                acc = term if acc is None else acc + term
            cbuf[r0:r0 + row_chunk, s * LANES:(s + 1) * LANES] = acc
        return run

    def norm_conv():
        c = cbuf[...] + bdw_ref[...]
        mu = jnp.mean(c, axis=-1, keepdims=True)
        var = jnp.mean(jnp.square(c - mu), axis=-1, keepdims=True)
        c = (c - mu) * lax.rsqrt(var + EPS) * lng_ref[...] + lnb_ref[...]
        cbf[...] = (c * (1.0 / (1.0 + jnp.exp(-c)))).astype(BF16)

    def pointwise():
        yb = jnp.dot(cbf[...], wpw_ref[...], preferred_element_type=F32)
        ycat[:, 0:pw] = ya_new[...].astype(BF16)
        ycat[:, pw:pw + cw] = _rms(yb, gb_ref[...]).astype(BF16)

    def out_cols(c0):
        def run():
            o_ref[:, c0:c0 + col] = x_ref[:, c0:c0 + col] + jnp.dot(
                ycat[...], wout_ref[:, c0:c0 + col], preferred_element_type=F32)
        return run

    norm_in()
    for c0 in range(0, cw, col):
        if c0 < pw:
            proj_pool(c0)()
        proj_glu(c0)()
        if c0 < pw:
            pool_group(c0 // gdim, POOL_WINDOWS[c0 // gdim])()
        for s in range(c0 // LANES, (c0 + col) // LANES):
            for r0 in range(0, tm, row_chunk):
                conv_chunk(r0, s)()
    norm_pool()
    norm_conv()
    pointwise()
    for c0 in range(0, x_ref.shape[1], col):
        out_cols(c0)()
    keep_halo(ubuf, POOL_HALO)
    keep_halo(ybuf, CONV_HALO)


def _mixer(x, g_mix, w_in, w_pool, pool_scale, w_dw, b_dw, ln_g, ln_b, w_pw, g_a, g_b, w_out):
    n, d = x.shape
    pw = w_pool.shape[0] * w_pool.shape[1]
    cw = w_pw.shape[0]
    tm = min(MIXER_TILE, n)
    assert cw % (pw // len(POOL_WINDOWS)) == 0 and pw <= cw
    row = pl.BlockSpec((tm, d), lambda j: (j, 0))
    return pl.pallas_call(
        _mixer_kernel,
        grid=(n // tm,),
        in_specs=[row, _resident(g_mix.shape), _resident(w_in.shape), _resident(w_pool.shape),
                  _resident(pool_scale.shape), _resident(w_dw.shape), _resident(b_dw.shape),
                  _resident(ln_g.shape), _resident(ln_b.shape), _resident(w_pw.shape),
                  _resident(g_a.shape), _resident(g_b.shape), _resident(w_out.shape)],
        out_specs=row,
        out_shape=jax.ShapeDtypeStruct((n, d), F32),
        scratch_shapes=[pltpu.VMEM((tm, d), BF16),
                        pltpu.VMEM((pw // LANES, TIME_PITCH * (tm + POOL_HALO), LANES), F32),
                        pltpu.VMEM((cw // LANES, TIME_PITCH * (tm + CONV_HALO), LANES), F32),
                        pltpu.VMEM((tm, cw), F32),
                        pltpu.VMEM((tm, cw), BF16),
                        pltpu.VMEM((tm, pw), F32),
                        pltpu.VMEM((tm, pw + cw), BF16)],
        compiler_params=pltpu.CompilerParams(dimension_semantics=("arbitrary",),
                                             vmem_limit_bytes=VMEM_LIMIT_BYTES),
        name="mixer",
    )(x, g_mix, w_in, w_pool, pool_scale, w_dw, b_dw, ln_g, ln_b, w_pw, g_a, g_b, w_out)


def _memory_kernel(mem_ref, g_ref, wkv_ref, wq_ref, wo_ref, a_ref, b_ref):
    j = pl.program_id(0)
    m = _rms(mem_ref[...], g_ref[...]).astype(BF16)
    kv = jnp.dot(m, wkv_ref[...].astype(BF16), preferred_element_type=F32).astype(BF16)

    @pl.when(j < XATTN_HEADS)
    def _():
        a_ref[...] = lax.dot_general(wq_ref[...].astype(BF16), kv, (((1,), (1,)), ((), ())),
                                     preferred_element_type=F32).astype(BF16)

    @pl.when(j >= XATTN_HEADS)
    def _():
        b_ref[...] = jnp.dot(kv, wo_ref[...].astype(BF16), preferred_element_type=F32).astype(BF16)


def _memory(mem, g_mem, w_kv, w_q, w_o):
    m, d = mem.shape
    nh = XATTN_HEADS
    hd = d // nh
    return pl.pallas_call(
        _memory_kernel,
        grid=(2 * nh,),
        in_specs=[_resident((m, d)), _resident(g_mem.shape),
                  pl.BlockSpec((d, hd), lambda j: (0, j)),
                  pl.BlockSpec((d, hd), lambda j: (0, jnp.minimum(j, nh - 1))),
                  pl.BlockSpec((hd, d), lambda j: (jnp.maximum(j - nh, 0), 0))],
        out_specs=[pl.BlockSpec((d, m), lambda j: (0, jnp.minimum(j, nh - 1))),
                   pl.BlockSpec((m, d), lambda j: (jnp.maximum(j - nh, 0), 0))],
        out_shape=[jax.ShapeDtypeStruct((d, nh * m), BF16), jax.ShapeDtypeStruct((nh * m, d), BF16)],
        compiler_params=pltpu.CompilerParams(dimension_semantics=("arbitrary",),
                                             vmem_limit_bytes=VMEM_LIMIT_BYTES),
        name="memory",
    )(mem, g_mem, w_kv, w_q, w_o)


def _xattn_kernel(x_ref, gx_ref, a_ref, b_ref, gf_ref, wr_ref, o_ref, lg_ref):
    d = x_ref.shape[1]
    hd = d // XATTN_HEADS
    m = a_ref.shape[1] // XATTN_HEADS
    x = x_ref[...]
    h = _rms(x, gx_ref[...])
    s_all = jnp.dot(h.astype(BF16), a_ref[...], preferred_element_type=F32)
    ps = []
    for hh in range(XATTN_HEADS):
        s = s_all[:, hh * m:(hh + 1) * m] * (hd ** -0.5)
        e = jnp.exp(s - jnp.max(s, axis=-1, keepdims=True))
        ps.append((e / jnp.sum(e, axis=-1, keepdims=True)).astype(BF16))
    x2 = x + jnp.dot(jnp.concatenate(ps, axis=-1), b_ref[...], preferred_element_type=F32)
    _store_token_major(o_ref, x2)
    h3 = _rms(x2, gf_ref[...])
    lg_ref[...] = lax.dot_general(wr_ref[...], h3.astype(BF16), (((1,), (1,)), ((), ())),
                                  preferred_element_type=F32)


def _xattn(x, g_x, qk, vo, g_ffn, w_router_t):
    n, d = x.shape
    tm = min(XATTN_TILE, n)
    chunks = d // LANES
    row = pl.BlockSpec((tm, d), lambda i: (i, 0))
    return pl.pallas_call(
        _xattn_kernel,
        grid=(n // tm,),
        in_specs=[row, _resident(g_x.shape), _resident(qk.shape), _resident(vo.shape),
                  _resident(g_ffn.shape), _resident(w_router_t.shape)],
        out_specs=[pl.BlockSpec((tm * chunks, LANES), lambda i: (i, 0)),
                   pl.BlockSpec((ROUTE_ROWS, tm), lambda i: (0, i))],
        out_shape=[jax.ShapeDtypeStruct((n * chunks, LANES), F32), jax.ShapeDtypeStruct((ROUTE_ROWS, n), F32)],
        compiler_params=pltpu.CompilerParams(dimension_semantics=("arbitrary",),
                                             vmem_limit_bytes=VMEM_LIMIT_BYTES),
        name="xattn",
    )(x, g_x, qk, vo, g_ffn, w_router_t)


def _first_argmax(v, rows):
    m = jnp.max(v, axis=0, keepdims=True)
    idx = jnp.min(jnp.where(v == m, rows.astype(F32), float(v.shape[0])), axis=0, keepdims=True)
    return m, idx.astype(I32)


def _route_kernel(lg_ref, bias_ref, tri_ref, etri_ref, dest_ref, wtok_ref, blk_ref, bst_ref, eg_buf, rank_buf,
                  *, block):
    n = lg_ref.shape[1]
    t = tri_ref.shape[0]
    n_tiles = n // t
    epg = EXPERTS_PER_GROUP
    rows_g = lax.broadcasted_iota(I32, (N_GROUPS, t), 0)
    rows_e = lax.broadcasted_iota(I32, (epg, t), 0)
    rows_all = lax.broadcasted_iota(I32, (N_EXPERTS, t), 0)

    def tile_body(j, carry):
        sl = pl.ds(pl.multiple_of(j * t, t), t)
        le = lg_ref[0:N_EXPERTS, sl] + bias_ref[0:N_EXPERTS, :]
        lgp = lg_ref[N_EXPERTS:N_EXPERTS + N_GROUPS, sl] + bias_ref[N_EXPERTS:N_EXPERTS + N_GROUPS, :]
        eg_ = jnp.exp(lgp - jnp.max(lgp, axis=0, keepdims=True))
        pg = eg_ / jnp.sum(eg_, axis=0, keepdims=True)
        pg_top, g_idx = _first_argmax(pg, rows_g)
        le_sel = jnp.zeros((epg, t), F32)
        for g in range(N_GROUPS):
            le_sel = jnp.where(g_idx == g, le[g * epg:(g + 1) * epg, :], le_sel)
        ee = jnp.exp(le_sel - jnp.max(le_sel, axis=0, keepdims=True))
        pe = ee / jnp.sum(ee, axis=0, keepdims=True)
        p1, i1 = _first_argmax(pe, rows_e)
        p2, i2 = _first_argmax(jnp.where(rows_e == i1, -1.0, pe), rows_e)
        psum = p1 + p2
        w1 = pg_top * p1 / psum
        w2 = pg_top * p2 / psum
        e1 = g_idx * epg + i1
        e2 = g_idx * epg + i2
        oh1 = rows_all == e1
        oh2 = rows_all == e2
        onehot = jnp.where(oh1 | oh2, 1.0, 0.0)
        prefix = carry + jnp.dot(onehot.astype(BF16), tri_ref[...], preferred_element_type=F32)
        r1 = jnp.sum(jnp.where(oh1, prefix, 0.0), axis=0, keepdims=True)
        r2 = jnp.sum(jnp.where(oh2, prefix, 0.0), axis=0, keepdims=True)
        eg_buf[0:1, sl] = e1
        eg_buf[1:2, sl] = e2
        rank_buf[0:1, sl] = r1
        rank_buf[1:2, sl] = r2
        wtok_ref[0:1, sl] = w1
        wtok_ref[1:2, sl] = w2
        return carry + jnp.sum(onehot, axis=1, keepdims=True)

    counts = lax.fori_loop(0, n_tiles, tile_body, jnp.zeros((N_EXPERTS, 1), F32))
    wtok_ref[2:SUBLANES, :] = jnp.zeros((SUBLANES - 2, n), F32)

    nblk = jnp.floor((counts + (block - 1)) * (1.0 / block))
    nblk_b = jnp.broadcast_to(nblk, (N_EXPERTS, LANES)).astype(BF16)
    bstart = jnp.dot(etri_ref[...], nblk_b, preferred_element_type=F32)[:, 0:1]
    bend = bstart + nblk
    pstart = bstart * block

    def dest_body(j, _):
        sl = pl.ds(pl.multiple_of(j * t, t), t)
        for k in range(TOP_K):
            oh = rows_all == eg_buf[k:k + 1, sl]
            base = jnp.sum(jnp.where(oh, pstart, 0.0), axis=0, keepdims=True)
            dest_ref[k:k + 1, sl] = (base + rank_buf[k:k + 1, sl]).astype(I32)
        return 0

    lax.fori_loop(0, n_tiles, dest_body, 0)
    dest_ref[2:SUBLANES, :] = jnp.zeros((SUBLANES - 2, n), I32)

    nbp = blk_ref.shape[1]
    bidx = lax.broadcasted_iota(I32, (N_EXPERTS, nbp), 1).astype(F32)
    be = jnp.sum(jnp.where(bend <= bidx, 1.0, 0.0), axis=0, keepdims=True)
    n_used = jnp.max(bend, axis=0, keepdims=True)
    blk_ref[0:1, :] = be.astype(I32)
    blk_ref[1:2, :] = jnp.broadcast_to(n_used, (1, nbp)).astype(I32)
    blk_ref[2:SUBLANES, :] = jnp.zeros((SUBLANES - 2, nbp), I32)
    lane = lax.broadcasted_iota(I32, bst_ref.shape, 1)
    bst_ref[...] = jnp.where(lane == 1, counts, bstart).astype(I32)
    eg_buf[2:SUBLANES, :] = jnp.zeros((SUBLANES - 2, n), I32)


def _route(logits_t, bias, block, nb):
    n = logits_t.shape[1]
    t = min(ROUTE_TILE, n)
    nbp = pl.cdiv(nb, LANES) * LANES
    assert nbp <= 256, "block counts must stay exactly representable in bf16"
    tri = (lax.broadcasted_iota(I32, (t, t), 0) < lax.broadcasted_iota(I32, (t, t), 1)).astype(BF16)
    etri = (lax.broadcasted_iota(I32, (N_EXPERTS, N_EXPERTS), 1)
            < lax.broadcasted_iota(I32, (N_EXPERTS, N_EXPERTS), 0)).astype(BF16)
    return pl.pallas_call(
        functools.partial(_route_kernel, block=block),
        grid=(1,),
        in_specs=[pl.BlockSpec((ROUTE_ROWS, n), lambda i: (0, 0)), _resident(bias.shape), _resident(tri.shape),
                  _resident(etri.shape)],
        out_specs=[pl.BlockSpec((SUBLANES, n), lambda i: (0, 0)), pl.BlockSpec((SUBLANES, n), lambda i: (0, 0)),
                   pl.BlockSpec((SUBLANES, nbp), lambda i: (0, 0)),
                   pl.BlockSpec((N_EXPERTS, LANES), lambda i: (0, 0)),
                   pl.BlockSpec((SUBLANES, n), lambda i: (0, 0))],
        out_shape=[jax.ShapeDtypeStruct((SUBLANES, n), I32), jax.ShapeDtypeStruct((SUBLANES, n), F32),
                   jax.ShapeDtypeStruct((SUBLANES, nbp), I32), jax.ShapeDtypeStruct((N_EXPERTS, LANES), I32),
                   jax.ShapeDtypeStruct((SUBLANES, n), I32)],
        scratch_shapes=[pltpu.VMEM((SUBLANES, n), F32)],
        compiler_params=pltpu.CompilerParams(dimension_semantics=("arbitrary",),
                                             vmem_limit_bytes=VMEM_LIMIT_BYTES),
        name="route",
    )(logits_t, bias, tri, etri)


def _start_token_gather(tok, src_hbm, dst, sem, row0, chunks):
    pltpu.make_async_copy(src_hbm.at[pl.ds(pl.multiple_of(tok * chunks, chunks), chunks)],
                          dst.at[pl.ds(row0, chunks)], sem).start()


def _gather_tokens(token_of, src_hbm, dst, sem, count, chunks, unrolled):
    pitch = _gather_pitch(chunks)
    if unrolled:
        for r in range(count):
            _start_token_gather(token_of(r), src_hbm, dst, sem, r * pitch, chunks)
    else:
        def body(r, _):
            _start_token_gather(token_of(r), src_hbm, dst, sem, pl.multiple_of(r * pitch, SUBLANES), chunks)
            return 0
        lax.fori_loop(0, count, body, 0, unroll=8)


def _wait_token_gathers(src_hbm, dst, sem, count, chunks):
    pltpu.make_async_copy(src_hbm.at[pl.ds(0, count * chunks)], dst.at[pl.ds(0, count * chunks)], sem).wait()


def _experts_kernel(bnd_ref, base_ref, tok_ref, x_hbm, gf_ref, wg_hbm, wu_hbm, wd_hbm, y_hbm,
                    xg, gsem, obuf, osem, zbuf, zsem, wg32, wu32, wd32, wsem, wg16, wu16, wd16, *, rows, n_blocks):
    e = pl.program_id(0)
    ne = pl.num_programs(0)
    block_rows = obuf.shape[1]
    chunks = block_rows // rows
    pitch = _gather_pitch(chunks)
    n_used = bnd_ref[ne]

    def gather(b, slot, unrolled):
        base = base_ref[b]
        _gather_tokens(lambda r: tok_ref[base + r], x_hbm, xg.at[slot], gsem.at[slot], rows, chunks, unrolled)

    def out_copy(b, slot):
        dst = y_hbm.at[pl.ds(pl.multiple_of(b * block_rows, block_rows), block_rows)]
        return pltpu.make_async_copy(obuf.at[slot], dst, osem.at[slot])

    def weight_copies(expert, slot):
        return [pltpu.make_async_copy(hbm.at[expert], buf.at[slot], wsem.at[slot])
                for hbm, buf in ((wg_hbm, wg32), (wu_hbm, wu32), (wd_hbm, wd32))]

    def zero_copy(b):
        dst = y_hbm.at[pl.ds(pl.multiple_of(b * block_rows, block_rows), block_rows)]
        return pltpu.make_async_copy(zbuf, dst, zsem.at[0])

    @pl.when(e == 0)
    def _():
        for cp in weight_copies(0, 0):
            cp.start(priority=1)
        gather(0, 0, unrolled=False)
        zbuf[...] = jnp.zeros(zbuf.shape, F32)

        def fill(b, carry):
            zero_copy(b).start(priority=1)
            return carry

        lax.fori_loop(n_used, n_blocks, fill, 0)

    @pl.when(e + 1 < ne)
    def _():
        for cp in weight_copies(e + 1, (e + 1) % 2):
            cp.start(priority=1)

    for cp in weight_copies(e, e % 2):
        cp.wait()

    @pl.when(bnd_ref[e + 1] > bnd_ref[e])
    def _():
        wg16[...] = wg32[e % 2].astype(BF16)
        wu16[...] = wu32[e % 2].astype(BF16)
        wd16[...] = wd32[e % 2].astype(BF16)

    def block_body(b, carry):
        slot = b % 2

        @pl.when(b >= 2)
        def _():
            out_copy(b - 2, slot).wait()

        _wait_token_gathers(x_hbm, xg.at[slot], gsem.at[slot], rows, chunks)
        xb = _load_token_major(xg.at[slot], rows, chunks, pitch)
        gather(jnp.minimum(b + 1, n_blocks - 1), 1 - slot, unrolled=True)
        h = _rms(xb, gf_ref[...]).astype(BF16)
        hg = jnp.dot(h, wg16[...], preferred_element_type=F32)
        hu = jnp.dot(h, wu16[...], preferred_element_type=F32)
        hid = hg * (1.0 / (1.0 + jnp.exp(-hg))) * hu
        _store_token_major(obuf.at[slot], jnp.dot(hid.astype(BF16), wd16[...], preferred_element_type=F32))
        out_copy(b, slot).start()
        return carry

    lax.fori_loop(bnd_ref[e], bnd_ref[e + 1], block_body, 0)

    @pl.when(e == ne - 1)
    def _():
        _wait_token_gathers(x_hbm, xg.at[n_used % 2], gsem.at[n_used % 2], rows, chunks)

        @pl.when(n_used >= 2)
        def _():
            out_copy(n_used - 2, n_used % 2).wait()

        out_copy(n_used - 1, (n_used - 1) % 2).wait()

        def drain(b, carry):
            zero_copy(b).wait()
            return carry

        lax.fori_loop(n_used, n_blocks, drain, 0)


def _experts(bounds, block_base, sorted_tok, x2c, g_ffn, w_gate, w_up, w_down, block):
    n_blocks = block_base.shape[0]
    ne, d, de = w_gate.shape
    chunks = d // LANES
    grid_spec = pltpu.PrefetchScalarGridSpec(
        num_scalar_prefetch=3,
        grid=(ne,),
        in_specs=[pl.BlockSpec(memory_space=pl.ANY),
                  _resident(g_ffn.shape),
                  pl.BlockSpec(memory_space=pl.ANY),
                  pl.BlockSpec(memory_space=pl.ANY),
                  pl.BlockSpec(memory_space=pl.ANY)],
        out_specs=pl.BlockSpec(memory_space=pl.ANY),
        scratch_shapes=[pltpu.VMEM((2, block * _gather_pitch(chunks), LANES), F32), pltpu.SemaphoreType.DMA((2,)),
                        pltpu.VMEM((2, block * chunks, LANES), F32), pltpu.SemaphoreType.DMA((2,)),
                        pltpu.VMEM((block * chunks, LANES), F32), pltpu.SemaphoreType.DMA((1,)),
                        pltpu.VMEM((2, d, de), F32), pltpu.VMEM((2, d, de), F32), pltpu.VMEM((2, de, d), F32),
                        pltpu.SemaphoreType.DMA((2,)),
                        pltpu.VMEM((d, de), BF16), pltpu.VMEM((d, de), BF16), pltpu.VMEM((de, d), BF16)],
    )
    return pl.pallas_call(
        functools.partial(_experts_kernel, rows=block, n_blocks=n_blocks),
        grid_spec=grid_spec,
        out_shape=jax.ShapeDtypeStruct((n_blocks * block * chunks, LANES), F32),
        compiler_params=pltpu.CompilerParams(dimension_semantics=("arbitrary",),
                                             vmem_limit_bytes=VMEM_LIMIT_BYTES),
        name="experts",
    )(bounds, block_base, sorted_tok, x2c, g_ffn, w_gate, w_up, w_down)


def _combine_kernel(d_ref, d1_ref, d2_ref, x_ref, w_ref, y_hbm, g_ref, o_ref, ybuf, sems):
    i = pl.program_id(0)
    nt = pl.num_programs(0)
    tm, d = o_ref.shape
    chunks = d // LANES
    depth = ybuf.shape[0]
    slot = i % depth

    def expert_rows(k):
        return _load_token_major(ybuf.at[slot, k], tm, chunks, _gather_pitch(chunks))

    def start(idx_ref, s, unrolled=False):
        for k in range(TOP_K):
            _gather_tokens(lambda r, k=k: idx_ref[0, 0, k * tm + r], y_hbm, ybuf.at[s, k], sems.at[s], tm, chunks,
                           unrolled=unrolled)

    def wait(s):
        for k in range(TOP_K):
            _wait_token_gathers(y_hbm, ybuf.at[s, k], sems.at[s], tm, chunks)

    @pl.when(i == 0)
    def _():
        start(d_ref, 0)

    @pl.when(jnp.logical_and(i == 0, nt > 1))
    def _():
        start(d1_ref, 1)

    @pl.when(i + 2 < nt)
    def _():
        start(d2_ref, (i + 2) % depth, unrolled=True)

    wait(slot)
    w = w_ref[...]
    y = _load_token_major(x_ref, tm, chunks, chunks) + (expert_rows(0) * w[:, 0:1] + expert_rows(1) * w[:, 1:2])
    o_ref[...] = _rms(y, g_ref[...])


def _combine(dest_tiles, x2c, w_tok, ysc, g_final):
    d = g_final.shape[1]
    chunks = d // LANES
    n = x2c.shape[0] // chunks
    tm = dest_tiles.shape[2] // TOP_K
    nt = n // tm
    smem_blk = lambda f: pl.BlockSpec((1, 1, TOP_K * tm), f, memory_space=pltpu.SMEM)
    return pl.pallas_call(
        _combine_kernel,
        grid=(nt,),
        in_specs=[smem_blk(lambda i: (i, 0, 0)), smem_blk(lambda i: (jnp.minimum(i + 1, nt - 1), 0, 0)),
                  smem_blk(lambda i: (jnp.minimum(i + 2, nt - 1), 0, 0)),
                  pl.BlockSpec((tm * chunks, LANES), lambda i: (i, 0)),
                  pl.BlockSpec((tm, SUBLANES), lambda i: (i, 0)), pl.BlockSpec(memory_space=pl.ANY),
                  _resident(g_final.shape)],
        out_specs=pl.BlockSpec((tm, d), lambda i: (i, 0)),
        out_shape=jax.ShapeDtypeStruct((n, d), F32),
        scratch_shapes=[pltpu.VMEM((COMBINE_DEPTH, TOP_K, tm * _gather_pitch(chunks), LANES), F32),
                        pltpu.SemaphoreType.DMA((COMBINE_DEPTH,))],
        compiler_params=pltpu.CompilerParams(dimension_semantics=("arbitrary",),
                                             vmem_limit_bytes=VMEM_LIMIT_BYTES),
        name="combine",
    )(dest_tiles, dest_tiles, dest_tiles, x2c, w_tok, ysc, g_final)


def _layer(x, mem, norm_mix_g, w_in, w_pool, pool_scale, w_dw, b_dw, conv_ln_g, conv_ln_b, w_conv_pw,
           out_norm_a_g, out_norm_b_g, w_out, norm_xattn_g, norm_mem_g, w_q_mem, w_kv_mem, w_o_mem, norm_ffn_g,
           w_router_group, b_router_group, w_router_expert, b_router_expert, w_exp_gate, w_exp_up, w_exp_down,
           out_gain):
    n, d = x.shape
    r2 = lambda v: v.reshape(1, -1)
    cw = w_conv_pw.shape[0]
    x1 = _mixer(x, r2(norm_mix_g), w_in.astype(BF16), w_pool.astype(BF16), r2(pool_scale),
                w_dw.reshape(CONV_KERNEL, cw), r2(b_dw),
                r2(conv_ln_g), r2(conv_ln_b), w_conv_pw.astype(BF16), r2(out_norm_a_g), r2(out_norm_b_g),
                w_out.astype(BF16))

    qk, vo = _memory(mem, r2(norm_mem_g), w_kv_mem, w_q_mem, w_o_mem)
    pad_rows = ROUTE_ROWS - N_EXPERTS - N_GROUPS
    w_router_t = jnp.concatenate([w_router_expert.T, w_router_group.T, jnp.zeros((pad_rows, d), F32)]).astype(BF16)
    x2, logits_t = _xattn(x1, r2(norm_xattn_g), qk, vo, r2(norm_ffn_g), w_router_t)

    block = MOE_BLOCK
    nb = pl.cdiv(n * TOP_K, block) + N_EXPERTS
    bias = jnp.concatenate([b_router_expert, b_router_group, jnp.zeros((pad_rows,), F32)]).reshape(ROUTE_ROWS, 1)
    dest, w_tok, blk, bst, eg = _route(logits_t, bias, block, nb)
    dest = dest[:TOP_K]
    n_used = blk[1, 0]
    bounds = jnp.concatenate([bst[:, 0], blk[1, :1]])
    keys = jnp.sort((eg[:TOP_K] * n + jnp.arange(n, dtype=I32)).reshape(-1))
    sorted_tok = jnp.concatenate([keys % n, jnp.arange(block, dtype=I32)])
    first_row = jnp.cumsum(bst[:, 1]) - bst[:, 1]
    bidx = jnp.arange(nb, dtype=I32)
    block_e = jnp.minimum(blk[0, :nb], N_EXPERTS - 1)
    block_base = jnp.where(bidx < n_used, first_row[block_e] + (bidx - bst[block_e, 0]) * block, 0)

    ys = _experts(bounds, block_base, sorted_tok, x2, r2(norm_ffn_g), w_exp_gate, w_exp_up, w_exp_down, block)

    tm = min(COMBINE_TILE, n)
    dest_tiles = dest.reshape(TOP_K, n // tm, tm).transpose(1, 0, 2).reshape(n // tm, 1, TOP_K * tm)
    return _combine(dest_tiles, x2, w_tok.T, ys, r2(out_gain))


def kernel(x, mem, norm_mix_g, w_in, w_pool, pool_scale, w_dw, b_dw, conv_ln_g, conv_ln_b, w_conv_pw, out_norm_a_g,
           out_norm_b_g, w_out, norm_xattn_g, norm_mem_g, w_q_mem, w_kv_mem, w_o_mem, norm_ffn_g, w_router_group,
           b_router_group, w_router_expert, b_router_expert, w_exp_gate, w_exp_up, w_exp_down, final_norm_g):
    assert x.shape[0] == 1 and mem.shape[0] == 1 and norm_mix_g.shape[0] == 1
    out = _layer(x[0], mem[0], norm_mix_g[0], w_in[0], w_pool[0], pool_scale[0], w_dw[0], b_dw[0], conv_ln_g[0],
                 conv_ln_b[0], w_conv_pw[0], out_norm_a_g[0], out_norm_b_g[0], w_out[0], norm_xattn_g[0],
                 norm_mem_g[0], w_q_mem[0], w_kv_mem[0], w_o_mem[0], norm_ffn_g[0], w_router_group[0],
                 b_router_group[0], w_router_expert[0], b_router_expert[0], w_exp_gate[0], w_exp_up[0],
                 w_exp_down[0], final_norm_g)
    return out[None]
```

```python
import functools

import jax
import jax.numpy as jnp
from jax import lax
from jax.experimental import pallas as pl
from jax.experimental.pallas import tpu as pltpu

F32 = jnp.float32
BF16 = jnp.bfloat16
I32 = jnp.int32

EPS = 1e-6
POOL_WINDOWS = (2, 4, 8, 16)
CONV_KERNEL = 31
XATTN_HEADS = 4
N_GROUPS = 4
EXPERTS_PER_GROUP = 8
N_EXPERTS = N_GROUPS * EXPERTS_PER_GROUP
TOP_K = 2

SUBLANES = 8
LANES = 128
VMEM_LIMIT_BYTES = 56 * 1024 * 1024

POOL_HALO = 16
TIME_PITCH = 2
CONV_HALO = 32
MIXER_TILE = 512
XATTN_TILE = 1024
ROUTE_TILE = 512
MOE_BLOCK = 256
GATHER_DEPTH = 3
COMBINE_TILE = 256
COMBINE_DEPTH = 3
ROUTE_ROWS = 40


def _rms(x, g):
    return x * lax.rsqrt(jnp.mean(x * x, axis=-1, keepdims=True) + EPS) * g


def _resident(shape):
    nd = len(shape)
    return pl.BlockSpec(shape, lambda *_: (0,) * nd, pipeline_mode=pl.Buffered(1))


def _store_token_major(ref, val):
    t, d = val.shape
    chunks = d // LANES
    for c in range(chunks):
        ref[pl.ds(c, t, stride=chunks), :] = val[:, c * LANES:(c + 1) * LANES]


def _load_token_major(ref, t, chunks, pitch):
    return jnp.concatenate([ref[pl.ds(c, t, stride=pitch), :] for c in range(chunks)], axis=-1)


def _gather_pitch(chunks):
    tiles = pl.cdiv(chunks, SUBLANES)
    return SUBLANES * (tiles + 1 - tiles % 2)


def _mixer_kernel(x_ref, gmix_ref, win_ref, wpool_ref, pscale_ref, wdw_ref, bdw_ref, lng_ref, lnb_ref,
                  wpw_ref, ga_ref, gb_ref, wout_ref, o_ref, hbuf, ubuf, ybuf, cbuf, cbf, ya_new, ycat):
    j = pl.program_id(0)
    tm = x_ref.shape[0]
    pw = ubuf.shape[0] * LANES
    cw = ybuf.shape[0] * LANES
    gdim = pw // len(POOL_WINDOWS)
    col = gdim

    @pl.when(j == 0)
    def _():
        ubuf[:, 0:TIME_PITCH * POOL_HALO, :] = jnp.zeros((pw // LANES, TIME_PITCH * POOL_HALO, LANES), F32)
        ybuf[:, 0:TIME_PITCH * CONV_HALO, :] = jnp.zeros((cw // LANES, TIME_PITCH * CONV_HALO, LANES), F32)

    def frames(buf, slab, r, count):
        return buf[slab, pl.ds(TIME_PITCH * r, count, stride=TIME_PITCH), :]

    def put_frames(buf, slab0, r, val):
        for s in range(val.shape[1] // LANES):
            buf[slab0 + s, pl.ds(TIME_PITCH * r, val.shape[0], stride=TIME_PITCH), :] = val[:, s * LANES:(s + 1) * LANES]

    def keep_halo(buf, halo):
        buf[:, 0:TIME_PITCH * halo, :] = buf[:, TIME_PITCH * tm:TIME_PITCH * (tm + halo), :]

    def norm_in():
        hbuf[...] = _rms(x_ref[...], gmix_ref[...]).astype(BF16)

    def proj_pool(c0):
        def run():
            u = jnp.dot(hbuf[...], win_ref[:, c0:c0 + col], preferred_element_type=F32)
            put_frames(ubuf, c0 // LANES, POOL_HALO, u)
        return run

    def proj_glu(c0):
        def run():
            a = jnp.dot(hbuf[...], win_ref[:, pw + c0:pw + c0 + col], preferred_element_type=F32)
            g = jnp.dot(hbuf[...], win_ref[:, pw + cw + c0:pw + cw + c0 + col], preferred_element_type=F32)
            put_frames(ybuf, c0 // LANES, CONV_HALO, a * (1.0 / (1.0 + jnp.exp(-g))))
        return run

    def pool_group(g, w):
        def run():
            pos = j * tm + lax.broadcasted_iota(I32, (tm, 1), 0)
            cols, toks = [], []
            for s in range(g * gdim // LANES, (g + 1) * gdim // LANES):
                tok = frames(ubuf, s, POOL_HALO, tm)
                wsum = tok
                for k in range(1, w):
                    wsum = wsum + frames(ubuf, s, POOL_HALO - k, tm)
                cols.append(wsum)
                toks.append(tok)
            cnt = jnp.minimum(pos + 1, w).astype(F32)
            d = jnp.concatenate(cols, axis=-1) / cnt - jnp.concatenate(toks, axis=-1)
            z = jnp.dot(d.astype(BF16), wpool_ref[g], preferred_element_type=F32)
            ya_new[:, g * gdim:(g + 1) * gdim] = z * pscale_ref[:, g * gdim:(g + 1) * gdim]
        return run

    def norm_pool():
        ya_new[...] = _rms(ya_new[...], ga_ref[...])

    first = CONV_HALO - (CONV_KERNEL - 1)
    row_chunk = 64

    def conv_chunk(r0, s):
        def run():
            acc = None
            for k in range(CONV_KERNEL):
                term = frames(ybuf, s, first + r0 + k, row_chunk) * wdw_ref[k:k + 1, s * LANES:(s + 1) * LANES]
                acc = term if acc is None else acc + term
            cbuf[r0:r0 + row_chunk, s * LANES:(s + 1) * LANES] = acc
        return run

    def norm_conv():
        c = cbuf[...] + bdw_ref[...]
        mu = jnp.mean(c, axis=-1, keepdims=True)
        var = jnp.mean(jnp.square(c - mu), axis=-1, keepdims=True)
        c = (c - mu) * lax.rsqrt(var + EPS) * lng_ref[...] + lnb_ref[...]
        cbf[...] = (c * (1.0 / (1.0 + jnp.exp(-c)))).astype(BF16)

    def pointwise():
        yb = jnp.dot(cbf[...], wpw_ref[...], preferred_element_type=F32)
        ycat[:, 0:pw] = ya_new[...].astype(BF16)
        ycat[:, pw:pw + cw] = _rms(yb, gb_ref[...]).astype(BF16)

    def out_cols(c0):
        def run():
            o_ref[:, c0:c0 + col] = x_ref[:, c0:c0 + col] + jnp.dot(
                ycat[...], wout_ref[:, c0:c0 + col], preferred_element_type=F32)
        return run

    norm_in()
    for c0 in range(0, cw, col):
        if c0 < pw:
            proj_pool(c0)()
        proj_glu(c0)()
        if c0 < pw:
            pool_group(c0 // gdim, POOL_WINDOWS[c0 // gdim])()
        for s in range(c0 // LANES, (c0 + col) // LANES):
            for r0 in range(0, tm, row_chunk):
                conv_chunk(r0, s)()
    norm_pool()
    norm_conv()
    pointwise()
    for c0 in range(0, x_ref.shape[1], col):
        out_cols(c0)()
    keep_halo(ubuf, POOL_HALO)
    keep_halo(ybuf, CONV_HALO)


def _mixer(x, g_mix, w_in, w_pool, pool_scale, w_dw, b_dw, ln_g, ln_b, w_pw, g_a, g_b, w_out):
    n, d = x.shape
    pw = w_pool.shape[0] * w_pool.shape[1]
    cw = w_pw.shape[0]
    tm = min(MIXER_TILE, n)
    assert cw % (pw // len(POOL_WINDOWS)) == 0 and pw <= cw
    row = pl.BlockSpec((tm, d), lambda j: (j, 0))
    return pl.pallas_call(
        _mixer_kernel,
        grid=(n // tm,),
        in_specs=[row, _resident(g_mix.shape), _resident(w_in.shape), _resident(w_pool.shape),
                  _resident(pool_scale.shape), _resident(w_dw.shape), _resident(b_dw.shape),
                  _resident(ln_g.shape), _resident(ln_b.shape), _resident(w_pw.shape),
                  _resident(g_a.shape), _resident(g_b.shape), _resident(w_out.shape)],
        out_specs=row,
        out_shape=jax.ShapeDtypeStruct((n, d), F32),
        scratch_shapes=[pltpu.VMEM((tm, d), BF16),
                        pltpu.VMEM((pw // LANES, TIME_PITCH * (tm + POOL_HALO), LANES), F32),
                        pltpu.VMEM((cw // LANES, TIME_PITCH * (tm + CONV_HALO), LANES), F32),
                        pltpu.VMEM((tm, cw), F32),
                        pltpu.VMEM((tm, cw), BF16),
                        pltpu.VMEM((tm, pw), F32),
                        pltpu.VMEM((tm, pw + cw), BF16)],
        compiler_params=pltpu.CompilerParams(dimension_semantics=("arbitrary",),
                                             vmem_limit_bytes=VMEM_LIMIT_BYTES),
        name="mixer",
    )(x, g_mix, w_in, w_pool, pool_scale, w_dw, b_dw, ln_g, ln_b, w_pw, g_a, g_b, w_out)


def _memory_kernel(mem_ref, g_ref, wkv_ref, wq_ref, wo_ref, a_ref, b_ref):
    j = pl.program_id(0)
    m = _rms(mem_ref[...], g_ref[...]).astype(BF16)
    kv = jnp.dot(m, wkv_ref[...].astype(BF16), preferred_element_type=F32).astype(BF16)

    @pl.when(j < XATTN_HEADS)
    def _():
        a_ref[...] = lax.dot_general(wq_ref[...].astype(BF16), kv, (((1,), (1,)), ((), ())),
                                     preferred_element_type=F32).astype(BF16)

    @pl.when(j >= XATTN_HEADS)
    def _():
        b_ref[...] = jnp.dot(kv, wo_ref[...].astype(BF16), preferred_element_type=F32).astype(BF16)


def _memory(mem, g_mem, w_kv, w_q, w_o):
    m, d = mem.shape
    nh = XATTN_HEADS
    hd = d // nh
    return pl.pallas_call(
        _memory_kernel,
        grid=(2 * nh,),
        in_specs=[_resident((m, d)), _resident(g_mem.shape),
                  pl.BlockSpec((d, hd), lambda j: (0, j)),
                  pl.BlockSpec((d, hd), lambda j: (0, jnp.minimum(j, nh - 1))),
                  pl.BlockSpec((hd, d), lambda j: (jnp.maximum(j - nh, 0), 0))],
        out_specs=[pl.BlockSpec((d, m), lambda j: (0, jnp.minimum(j, nh - 1))),
                   pl.BlockSpec((m, d), lambda j: (jnp.maximum(j - nh, 0), 0))],
        out_shape=[jax.ShapeDtypeStruct((d, nh * m), BF16), jax.ShapeDtypeStruct((nh * m, d), BF16)],
        compiler_params=pltpu.CompilerParams(dimension_semantics=("arbitrary",),
                                             vmem_limit_bytes=VMEM_LIMIT_BYTES),
        name="memory",
    )(mem, g_mem, w_kv, w_q, w_o)


def _xattn_kernel(x_ref, gx_ref, a_ref, b_ref, gf_ref, wr_ref, o_ref, lg_ref):
    d = x_ref.shape[1]
    hd = d // XATTN_HEADS
    m = a_ref.shape[1] // XATTN_HEADS
    x = x_ref[...]
    h = _rms(x, gx_ref[...])
    s_all = jnp.dot(h.astype(BF16), a_ref[...], preferred_element_type=F32)
    ps = []
    for hh in range(XATTN_HEADS):
        s = s_all[:, hh * m:(hh + 1) * m] * (hd ** -0.5)
        e = jnp.exp(s - jnp.max(s, axis=-1, keepdims=True))
        ps.append((e / jnp.sum(e, axis=-1, keepdims=True)).astype(BF16))
    x2 = x + jnp.dot(jnp.concatenate(ps, axis=-1), b_ref[...], preferred_element_type=F32)
    _store_token_major(o_ref, x2)
    h3 = _rms(x2, gf_ref[...])
    lg_ref[...] = lax.dot_general(wr_ref[...], h3.astype(BF16), (((1,), (1,)), ((), ())),
                                  preferred_element_type=F32)


def _xattn(x, g_x, qk, vo, g_ffn, w_router_t):
    n, d = x.shape
    tm = min(XATTN_TILE, n)
    chunks = d // LANES
    row = pl.BlockSpec((tm, d), lambda i: (i, 0))
    return pl.pallas_call(
        _xattn_kernel,
        grid=(n // tm,),
        in_specs=[row, _resident(g_x.shape), _resident(qk.shape), _resident(vo.shape),
                  _resident(g_ffn.shape), _resident(w_router_t.shape)],
        out_specs=[pl.BlockSpec((tm * chunks, LANES), lambda i: (i, 0)),
                   pl.BlockSpec((ROUTE_ROWS, tm), lambda i: (0, i))],
        out_shape=[jax.ShapeDtypeStruct((n * chunks, LANES), F32), jax.ShapeDtypeStruct((ROUTE_ROWS, n), F32)],
        compiler_params=pltpu.CompilerParams(dimension_semantics=("arbitrary",),
                                             vmem_limit_bytes=VMEM_LIMIT_BYTES),
        name="xattn",
    )(x, g_x, qk, vo, g_ffn, w_router_t)


def _first_argmax(v, rows):
    m = jnp.max(v, axis=0, keepdims=True)
    idx = jnp.min(jnp.where(v == m, rows.astype(F32), float(v.shape[0])), axis=0, keepdims=True)
    return m, idx.astype(I32)


def _route_kernel(lg_ref, bias_ref, tri_ref, etri_ref, dest_ref, wtok_ref, blk_ref, bst_ref, eg_buf, rank_buf,
                  *, block):
    n = lg_ref.shape[1]
    t = tri_ref.shape[0]
    n_tiles = n // t
    epg = EXPERTS_PER_GROUP
    rows_g = lax.broadcasted_iota(I32, (N_GROUPS, t), 0)
    rows_e = lax.broadcasted_iota(I32, (epg, t), 0)
    rows_all = lax.broadcasted_iota(I32, (N_EXPERTS, t), 0)

    def tile_body(j, carry):
        sl = pl.ds(pl.multiple_of(j * t, t), t)
        le = lg_ref[0:N_EXPERTS, sl] + bias_ref[0:N_EXPERTS, :]
        lgp = lg_ref[N_EXPERTS:N_EXPERTS + N_GROUPS, sl] + bias_ref[N_EXPERTS:N_EXPERTS + N_GROUPS, :]
        eg_ = jnp.exp(lgp - jnp.max(lgp, axis=0, keepdims=True))
        pg = eg_ / jnp.sum(eg_, axis=0, keepdims=True)
        pg_top, g_idx = _first_argmax(pg, rows_g)
        le_sel = jnp.zeros((epg, t), F32)
        for g in range(N_GROUPS):
            le_sel = jnp.where(g_idx == g, le[g * epg:(g + 1) * epg, :], le_sel)
        ee = jnp.exp(le_sel - jnp.max(le_sel, axis=0, keepdims=True))
        pe = ee / jnp.sum(ee, axis=0, keepdims=True)
        p1, i1 = _first_argmax(pe, rows_e)
        p2, i2 = _first_argmax(jnp.where(rows_e == i1, -1.0, pe), rows_e)
        psum = p1 + p2
        w1 = pg_top * p1 / psum
        w2 = pg_top * p2 / psum
        e1 = g_idx * epg + i1
        e2 = g_idx * epg + i2
        oh1 = rows_all == e1
        oh2 = rows_all == e2
        onehot = jnp.where(oh1 | oh2, 1.0, 0.0)
        prefix = carry + jnp.dot(onehot.astype(BF16), tri_ref[...], preferred_element_type=F32)
        r1 = jnp.sum(jnp.where(oh1, prefix, 0.0), axis=0, keepdims=True)
        r2 = jnp.sum(jnp.where(oh2, prefix, 0.0), axis=0, keepdims=True)
        eg_buf[0:1, sl] = e1
        eg_buf[1:2, sl] = e2
        rank_buf[0:1, sl] = r1
        rank_buf[1:2, sl] = r2
        wtok_ref[0:1, sl] = w1
        wtok_ref[1:2, sl] = w2
        return carry + jnp.sum(onehot, axis=1, keepdims=True)

    counts = lax.fori_loop(0, n_tiles, tile_body, jnp.zeros((N_EXPERTS, 1), F32))
    wtok_ref[2:SUBLANES, :] = jnp.zeros((SUBLANES - 2, n), F32)

    nblk = jnp.floor((counts + (block - 1)) * (1.0 / block))
    nblk_b = jnp.broadcast_to(nblk, (N_EXPERTS, LANES)).astype(BF16)
    bstart = jnp.dot(etri_ref[...], nblk_b, preferred_element_type=F32)[:, 0:1]
    bend = bstart + nblk
    pstart = bstart * block

    def dest_body(j, _):
        sl = pl.ds(pl.multiple_of(j * t, t), t)
        for k in range(TOP_K):
            oh = rows_all == eg_buf[k:k + 1, sl]
            base = jnp.sum(jnp.where(oh, pstart, 0.0), axis=0, keepdims=True)
            dest_ref[k:k + 1, sl] = (base + rank_buf[k:k + 1, sl]).astype(I32)
        return 0

    lax.fori_loop(0, n_tiles, dest_body, 0)
    dest_ref[2:SUBLANES, :] = jnp.zeros((SUBLANES - 2, n), I32)

    nbp = blk_ref.shape[1]
    bidx = lax.broadcasted_iota(I32, (N_EXPERTS, nbp), 1).astype(F32)
    be = jnp.sum(jnp.where(bend <= bidx, 1.0, 0.0), axis=0, keepdims=True)
    n_used = jnp.max(bend, axis=0, keepdims=True)
    blk_ref[0:1, :] = be.astype(I32)
    blk_ref[1:2, :] = jnp.broadcast_to(n_used, (1, nbp)).astype(I32)
    blk_ref[2:SUBLANES, :] = jnp.zeros((SUBLANES - 2, nbp), I32)
    lane = lax.broadcasted_iota(I32, bst_ref.shape, 1)
    bst_ref[...] = jnp.where(lane == 1, counts, bstart).astype(I32)
    eg_buf[2:SUBLANES, :] = jnp.zeros((SUBLANES - 2, n), I32)


def _route(logits_t, bias, block, nb):
    n = logits_t.shape[1]
    t = min(ROUTE_TILE, n)
    nbp = pl.cdiv(nb, LANES) * LANES
    assert nbp <= 256, "block counts must stay exactly representable in bf16"
    tri = (lax.broadcasted_iota(I32, (t, t), 0) < lax.broadcasted_iota(I32, (t, t), 1)).astype(BF16)
    etri = (lax.broadcasted_iota(I32, (N_EXPERTS, N_EXPERTS), 1)
            < lax.broadcasted_iota(I32, (N_EXPERTS, N_EXPERTS), 0)).astype(BF16)
    return pl.pallas_call(
        functools.partial(_route_kernel, block=block),
        grid=(1,),
        in_specs=[pl.BlockSpec((ROUTE_ROWS, n), lambda i: (0, 0)), _resident(bias.shape), _resident(tri.shape),
                  _resident(etri.shape)],
        out_specs=[pl.BlockSpec((SUBLANES, n), lambda i: (0, 0)), pl.BlockSpec((SUBLANES, n), lambda i: (0, 0)),
                   pl.BlockSpec((SUBLANES, nbp), lambda i: (0, 0)),
                   pl.BlockSpec((N_EXPERTS, LANES), lambda i: (0, 0)),
                   pl.BlockSpec((SUBLANES, n), lambda i: (0, 0))],
        out_shape=[jax.ShapeDtypeStruct((SUBLANES, n), I32), jax.ShapeDtypeStruct((SUBLANES, n), F32),
                   jax.ShapeDtypeStruct((SUBLANES, nbp), I32), jax.ShapeDtypeStruct((N_EXPERTS, LANES), I32),
                   jax.ShapeDtypeStruct((SUBLANES, n), I32)],
        scratch_shapes=[pltpu.VMEM((SUBLANES, n), F32)],
        compiler_params=pltpu.CompilerParams(dimension_semantics=("arbitrary",),
                                             vmem_limit_bytes=VMEM_LIMIT_BYTES),
        name="route",
    )(logits_t, bias, tri, etri)


def _start_token_gather(tok, src_hbm, dst, sem, row0, chunks):
    pltpu.make_async_copy(src_hbm.at[pl.ds(pl.multiple_of(tok * chunks, chunks), chunks)],
                          dst.at[pl.ds(row0, chunks)], sem).start()


def _gather_tokens(token_of, src_hbm, dst, sem, count, chunks, unrolled):
    pitch = _gather_pitch(chunks)
    if unrolled:
        for r in range(count):
            _start_token_gather(token_of(r), src_hbm, dst, sem, r * pitch, chunks)
    else:
        def body(r, _):
            _start_token_gather(token_of(r), src_hbm, dst, sem, pl.multiple_of(r * pitch, SUBLANES), chunks)
            return 0
        lax.fori_loop(0, count, body, 0, unroll=8)


def _wait_token_gathers(src_hbm, dst, sem, count, chunks):
    pltpu.make_async_copy(src_hbm.at[pl.ds(0, count * chunks)], dst.at[pl.ds(0, count * chunks)], sem).wait()


def _experts_kernel(bnd_ref, base_ref, tok_ref, x_hbm, gf_ref, wg_hbm, wu_hbm, wd_hbm, y_hbm,
                    xg, gsem, obuf, osem, zbuf, zsem, wg32, wu32, wd32, wsem, wg16, wu16, wd16, *, rows, n_blocks):
    e = pl.program_id(0)
    ne = pl.num_programs(0)
    block_rows = obuf.shape[1]
    chunks = block_rows // rows
    pitch = _gather_pitch(chunks)
    depth = xg.shape[0]
    n_used = bnd_ref[ne]

    def gather(b, slot, unrolled):
        base = base_ref[b]
        _gather_tokens(lambda r: tok_ref[base + r], x_hbm, xg.at[slot], gsem.at[slot], rows, chunks, unrolled)

    def out_copy(b, slot):
        dst = y_hbm.at[pl.ds(pl.multiple_of(b * block_rows, block_rows), block_rows)]
        return pltpu.make_async_copy(obuf.at[slot], dst, osem.at[slot])

    def weight_copies(expert, slot):
        return [pltpu.make_async_copy(hbm.at[expert], buf.at[slot], wsem.at[slot])
                for hbm, buf in ((wg_hbm, wg32), (wu_hbm, wu32), (wd_hbm, wd32))]

    def zero_copy(b):
        dst = y_hbm.at[pl.ds(pl.multiple_of(b * block_rows, block_rows), block_rows)]
        return pltpu.make_async_copy(zbuf, dst, zsem.at[0])

    @pl.when(e == 0)
    def _():
        for cp in weight_copies(0, 0):
            cp.start(priority=1)
        for first in range(depth - 1):
            gather(first, first, unrolled=False)
        zbuf[...] = jnp.zeros(zbuf.shape, F32)

        def fill(b, carry):
            zero_copy(b).start(priority=1)
            return carry

        lax.fori_loop(n_used, n_blocks, fill, 0)

    @pl.when(e + 1 < ne)
    def _():
        for cp in weight_copies(e + 1, (e + 1) % 2):
            cp.start(priority=1)

    for cp in weight_copies(e, e % 2):
        cp.wait()

    @pl.when(bnd_ref[e + 1] > bnd_ref[e])
    def _():
        wg16[...] = wg32[e % 2].astype(BF16)
        wu16[...] = wu32[e % 2].astype(BF16)
        wd16[...] = wd32[e % 2].astype(BF16)

    def block_body(b, carry):
        slot = b % 2

        @pl.when(b >= 2)
        def _():
            out_copy(b - 2, slot).wait()

        gslot = b % depth
        _wait_token_gathers(x_hbm, xg.at[gslot], gsem.at[gslot], rows, chunks)
        xb = _load_token_major(xg.at[gslot], rows, chunks, pitch)
        gather(jnp.minimum(b + depth - 1, n_blocks - 1), (b + depth - 1) % depth, unrolled=True)
        h = _rms(xb, gf_ref[...]).astype(BF16)
        hg = jnp.dot(h, wg16[...], preferred_element_type=F32)
        hu = jnp.dot(h, wu16[...], preferred_element_type=F32)
        hid = hg * (1.0 / (1.0 + jnp.exp(-hg))) * hu
        _store_token_major(obuf.at[slot], jnp.dot(hid.astype(BF16), wd16[...], preferred_element_type=F32))
        out_copy(b, slot).start()
        return carry

    lax.fori_loop(bnd_ref[e], bnd_ref[e + 1], block_body, 0)

    @pl.when(e == ne - 1)
    def _():
        for ahead in range(depth - 1):
            _wait_token_gathers(x_hbm, xg.at[(n_used + ahead) % depth], gsem.at[(n_used + ahead) % depth], rows, chunks)

        @pl.when(n_used >= 2)
        def _():
            out_copy(n_used - 2, n_used % 2).wait()

        out_copy(n_used - 1, (n_used - 1) % 2).wait()

        def drain(b, carry):
            zero_copy(b).wait()
            return carry

        lax.fori_loop(n_used, n_blocks, drain, 0)


def _experts(bounds, block_base, sorted_tok, x2c, g_ffn, w_gate, w_up, w_down, block):
    n_blocks = block_base.shape[0]
    ne, d, de = w_gate.shape
    chunks = d // LANES
    grid_spec = pltpu.PrefetchScalarGridSpec(
        num_scalar_prefetch=3,
        grid=(ne,),
        in_specs=[pl.BlockSpec(memory_space=pl.ANY),
                  _resident(g_ffn.shape),
                  pl.BlockSpec(memory_space=pl.ANY),
                  pl.BlockSpec(memory_space=pl.ANY),
                  pl.BlockSpec(memory_space=pl.ANY)],
        out_specs=pl.BlockSpec(memory_space=pl.ANY),
        scratch_shapes=[pltpu.VMEM((GATHER_DEPTH, block * _gather_pitch(chunks), LANES), F32),
                        pltpu.SemaphoreType.DMA((GATHER_DEPTH,)),
                        pltpu.VMEM((2, block * chunks, LANES), F32), pltpu.SemaphoreType.DMA((2,)),
                        pltpu.VMEM((block * chunks, LANES), F32), pltpu.SemaphoreType.DMA((1,)),
                        pltpu.VMEM((2, d, de), F32), pltpu.VMEM((2, d, de), F32), pltpu.VMEM((2, de, d), F32),
                        pltpu.SemaphoreType.DMA((2,)),
                        pltpu.VMEM((d, de), BF16), pltpu.VMEM((d, de), BF16), pltpu.VMEM((de, d), BF16)],
    )
    return pl.pallas_call(
        functools.partial(_experts_kernel, rows=block, n_blocks=n_blocks),
        grid_spec=grid_spec,
        out_shape=jax.ShapeDtypeStruct((n_blocks * block * chunks, LANES), F32),
        compiler_params=pltpu.CompilerParams(dimension_semantics=("arbitrary",),
                                             vmem_limit_bytes=VMEM_LIMIT_BYTES),
        name="experts",
    )(bounds, block_base, sorted_tok, x2c, g_ffn, w_gate, w_up, w_down)


def _combine_kernel(d_ref, d1_ref, d2_ref, x_ref, w_ref, y_hbm, g_ref, o_ref, ybuf, sems):
    i = pl.program_id(0)
    nt = pl.num_programs(0)
    tm, d = o_ref.shape
    chunks = d // LANES
    depth = ybuf.shape[0]
    slot = i % depth

    def expert_rows(k):
        return _load_token_major(ybuf.at[slot, k], tm, chunks, _gather_pitch(chunks))

    def start(idx_ref, s, unrolled=False):
        for k in range(TOP_K):
            _gather_tokens(lambda r, k=k: idx_ref[0, 0, k * tm + r], y_hbm, ybuf.at[s, k], sems.at[s], tm, chunks,
                           unrolled=unrolled)

    def wait(s):
        for k in range(TOP_K):
            _wait_token_gathers(y_hbm, ybuf.at[s, k], sems.at[s], tm, chunks)

    @pl.when(i == 0)
    def _():
        start(d_ref, 0)

    @pl.when(jnp.logical_and(i == 0, nt > 1))
    def _():
        start(d1_ref, 1)

    @pl.when(i + 2 < nt)
    def _():
        start(d2_ref, (i + 2) % depth, unrolled=True)

    wait(slot)
    w = w_ref[...]
    y = _load_token_major(x_ref, tm, chunks, chunks) + (expert_rows(0) * w[:, 0:1] + expert_rows(1) * w[:, 1:2])
    o_ref[...] = _rms(y, g_ref[...])


def _combine(dest_tiles, x2c, w_tok, ysc, g_final):
    d = g_final.shape[1]
    chunks = d // LANES
    n = x2c.shape[0] // chunks
    tm = dest_tiles.shape[2] // TOP_K
    nt = n // tm
    smem_blk = lambda f: pl.BlockSpec((1, 1, TOP_K * tm), f, memory_space=pltpu.SMEM)
    return pl.pallas_call(
        _combine_kernel,
        grid=(nt,),
        in_specs=[smem_blk(lambda i: (i, 0, 0)), smem_blk(lambda i: (jnp.minimum(i + 1, nt - 1), 0, 0)),
                  smem_blk(lambda i: (jnp.minimum(i + 2, nt - 1), 0, 0)),
                  pl.BlockSpec((tm * chunks, LANES), lambda i: (i, 0)),
                  pl.BlockSpec((tm, SUBLANES), lambda i: (i, 0)), pl.BlockSpec(memory_space=pl.ANY),
                  _resident(g_final.shape)],
        out_specs=pl.BlockSpec((tm, d), lambda i: (i, 0)),
        out_shape=jax.ShapeDtypeStruct((n, d), F32),
        scratch_shapes=[pltpu.VMEM((COMBINE_DEPTH, TOP_K, tm * _gather_pitch(chunks), LANES), F32),
                        pltpu.SemaphoreType.DMA((COMBINE_DEPTH,))],
        compiler_params=pltpu.CompilerParams(dimension_semantics=("arbitrary",),
                                             vmem_limit_bytes=VMEM_LIMIT_BYTES),
        name="combine",
    )(dest_tiles, dest_tiles, dest_tiles, x2c, w_tok, ysc, g_final)


def _layer(x, mem, norm_mix_g, w_in, w_pool, pool_scale, w_dw, b_dw, conv_ln_g, conv_ln_b, w_conv_pw,
           out_norm_a_g, out_norm_b_g, w_out, norm_xattn_g, norm_mem_g, w_q_mem, w_kv_mem, w_o_mem, norm_ffn_g,
           w_router_group, b_router_group, w_router_expert, b_router_expert, w_exp_gate, w_exp_up, w_exp_down,
           out_gain):
    n, d = x.shape
    r2 = lambda v: v.reshape(1, -1)
    cw = w_conv_pw.shape[0]
    x1 = _mixer(x, r2(norm_mix_g), w_in.astype(BF16), w_pool.astype(BF16), r2(pool_scale),
                w_dw.reshape(CONV_KERNEL, cw), r2(b_dw),
                r2(conv_ln_g), r2(conv_ln_b), w_conv_pw.astype(BF16), r2(out_norm_a_g), r2(out_norm_b_g),
                w_out.astype(BF16))

    qk, vo = _memory(mem, r2(norm_mem_g), w_kv_mem, w_q_mem, w_o_mem)
    pad_rows = ROUTE_ROWS - N_EXPERTS - N_GROUPS
    w_router_t = jnp.concatenate([w_router_expert.T, w_router_group.T, jnp.zeros((pad_rows, d), F32)]).astype(BF16)
    x2, logits_t = _xattn(x1, r2(norm_xattn_g), qk, vo, r2(norm_ffn_g), w_router_t)

    block = MOE_BLOCK
    nb = pl.cdiv(n * TOP_K, block) + N_EXPERTS
    bias = jnp.concatenate([b_router_expert, b_router_group, jnp.zeros((pad_rows,), F32)]).reshape(ROUTE_ROWS, 1)
    dest, w_tok, blk, bst, eg = _route(logits_t, bias, block, nb)
    dest = dest[:TOP_K]
    n_used = blk[1, 0]
    bounds = jnp.concatenate([bst[:, 0], blk[1, :1]])
    keys = jnp.sort((eg[:TOP_K] * n + jnp.arange(n, dtype=I32)).reshape(-1))
    sorted_tok = jnp.concatenate([keys % n, jnp.arange(block, dtype=I32)])
    first_row = jnp.cumsum(bst[:, 1]) - bst[:, 1]
    bidx = jnp.arange(nb, dtype=I32)
    block_e = jnp.minimum(blk[0, :nb], N_EXPERTS - 1)
    block_base = jnp.where(bidx < n_used, first_row[block_e] + (bidx - bst[block_e, 0]) * block, 0)

    ys = _experts(bounds, block_base, sorted_tok, x2, r2(norm_ffn_g), w_exp_gate, w_exp_up, w_exp_down, block)

    tm = min(COMBINE_TILE, n)
    dest_tiles = dest.reshape(TOP_K, n // tm, tm).transpose(1, 0, 2).reshape(n // tm, 1, TOP_K * tm)
    return _combine(dest_tiles, x2, w_tok.T, ys, r2(out_gain))


def kernel(x, mem, norm_mix_g, w_in, w_pool, pool_scale, w_dw, b_dw, conv_ln_g, conv_ln_b, w_conv_pw, out_norm_a_g,
           out_norm_b_g, w_out, norm_xattn_g, norm_mem_g, w_q_mem, w_kv_mem, w_o_mem, norm_ffn_g, w_router_group,
           b_router_group, w_router_expert, b_router_expert, w_exp_gate, w_exp_up, w_exp_down, final_norm_g):
    assert x.shape[0] == 1 and mem.shape[0] == 1 and norm_mix_g.shape[0] == 1
    out = _layer(x[0], mem[0], norm_mix_g[0], w_in[0], w_pool[0], pool_scale[0], w_dw[0], b_dw[0], conv_ln_g[0],
                 conv_ln_b[0], w_conv_pw[0], out_norm_a_g[0], out_norm_b_g[0], w_out[0], norm_xattn_g[0],
                 norm_mem_g[0], w_q_mem[0], w_kv_mem[0], w_o_mem[0], norm_ffn_g[0], w_router_group[0],
                 b_router_group[0], w_router_expert[0], b_router_expert[0], w_exp_gate[0], w_exp_up[0],
                 w_exp_down[0], final_norm_g)
    return out[None]
```
